```python
import math
import jax, jax.numpy as jnp
from jax import lax
import numpy as np

D_MODEL = 2048
BATCH = 1
SEQ = 16384
DEPTH = 2
DEC_BATCH = 2
DEC_SEQ = 8192
PAST_LEN = 128

GRID_W = 64
N_HEADS = 8
HEAD_DIM = 128
ATT_WIDTH = N_HEADS * HEAD_DIM
KH_MAX = 8
KW = 16
F_GROUPS = 4
F_GROUP_DIM = 256
F_WIDTH = F_GROUPS * F_GROUP_DIM
IN_WIDTH = 3 * ATT_WIDTH + F_WIDTH
D_FF = ((8 * D_MODEL + 3 * 256 - 1) // (3 * 256)) * 256
LN_EPS = 1e-5
DEEPNORM_ALPHA = (2.0 * DEPTH) ** 0.25
DEEPNORM_BETA = (8.0 * DEPTH) ** -0.25

kernel_name = "hybrid_natten_fnet_deepnorm_encoder"


def _layer_norm(x, g, b):
    xf = x.astype(jnp.float32)
    mu = jnp.mean(xf, axis=-1, keepdims=True)
    var = jnp.mean(jnp.square(xf - mu), axis=-1, keepdims=True)
    y = (xf - mu) * lax.rsqrt(var + LN_EPS)
    return (y * g.astype(jnp.float32) + b.astype(jnp.float32)).astype(x.dtype)


def _neighbourhood_attention(q, k, v, rpb):
    B, T, H, Dh = q.shape
    rows = T // GRID_W
    kh = min(KH_MAX, rows)
    qg = q.reshape(B, rows, GRID_W, H, Dh)
    kg = k.reshape(B, rows, GRID_W, H, Dh)
    vg = v.reshape(B, rows, GRID_W, H, Dh)
    cols = np.arange(GRID_W)
    col_start = np.clip(cols - KW // 2, 0, GRID_W - KW)
    col_idx = col_start[:, None] + np.arange(KW)[None, :]
    col_bias_idx = col_idx - cols[:, None] + (KW - 1)
    rpb_cols = rpb[:, :, col_bias_idx].astype(jnp.float32)
    scale = HEAD_DIM ** -0.5

    def one_row(r):
        r_start = jnp.clip(r - kh // 2, 0, rows - kh)
        k_rows = lax.dynamic_slice_in_dim(kg, r_start, kh, axis=1)
        v_rows = lax.dynamic_slice_in_dim(vg, r_start, kh, axis=1)
        k_win = k_rows[:, :, col_idx]
        v_win = v_rows[:, :, col_idx]
        row_bias_idx = r_start - r + (KH_MAX - 1) + jnp.arange(kh)
        bias = jnp.take(rpb_cols, row_bias_idx, axis=1)
        bias = jnp.transpose(bias, (0, 2, 1, 3))
        q_row = lax.dynamic_index_in_dim(qg, r, axis=1, keepdims=False)
        s = jnp.einsum('bchd,bicjhd->bhcij', q_row, k_win).astype(jnp.float32)
        s = s * scale + bias[None]
        p = jax.nn.softmax(s.reshape(B, H, GRID_W, kh * KW), axis=-1)
        p = p.reshape(B, H, GRID_W, kh, KW).astype(v.dtype)
        return jnp.einsum('bhcij,bicjhd->bchd', p, v_win)

    out = lax.map(one_row, jnp.arange(rows))
    return jnp.moveaxis(out, 0, 1).reshape(B, T, H * Dh)


def _fourier_mix(u):
    B, T, C = u.shape
    ug = u.reshape(B, T, F_GROUPS, F_GROUP_DIM).astype(jnp.float32)
    mixed = jnp.fft.fft2(ug, axes=(1, 3), norm="ortho").real
    return mixed.reshape(B, T, C).astype(u.dtype)


def _trunk_layer(x, w_in, rpb, w_att, w_four, w_gate, b_gate, w_out, ln1_g, ln1_b,
                 w_ffn_gate, w_ffn_up, w_ffn_down, ln2_g, ln2_b):
    B, T, _ = x.shape
    proj = x @ w_in
    q, k, v, u = jnp.split(proj, [ATT_WIDTH, 2 * ATT_WIDTH, 3 * ATT_WIDTH], axis=-1)
    q = q.reshape(B, T, N_HEADS, HEAD_DIM)
    k = k.reshape(B, T, N_HEADS, HEAD_DIM)
    v = v.reshape(B, T, N_HEADS, HEAD_DIM)
    a = _neighbourhood_attention(q, k, v, rpb) @ w_att
    f = _fourier_mix(u) @ w_four
    gates = jax.nn.sigmoid(x @ w_gate + b_gate)
    g_a, g_f = jnp.split(gates, 2, axis=-1)
    mixed = (g_a * a + g_f * f) @ w_out
    x = _layer_norm(DEEPNORM_ALPHA * x + mixed, ln1_g, ln1_b)
    h = jax.nn.silu(x @ w_ffn_gate) * (x @ w_ffn_up)
    x = _layer_norm(DEEPNORM_ALPHA * x + h @ w_ffn_down, ln2_g, ln2_b)
    return x


def _trunk(x, ln_in_g, ln_in_b, w_in, rpb, w_att, w_four, w_gate, b_gate, w_out,
           ln1_g, ln1_b, w_ffn_gate, w_ffn_up, w_ffn_down, ln2_g, ln2_b):
    x = _layer_norm(x, ln_in_g, ln_in_b)
    for l in range(DEPTH):
        x = _trunk_layer(x, w_in[l], rpb[l], w_att[l], w_four[l], w_gate[l], b_gate[l],
                         w_out[l], ln1_g[l], ln1_b[l], w_ffn_gate[l], w_ffn_up[l],
                         w_ffn_down[l], ln2_g[l], ln2_b[l])
    return x


def setup_inputs(seed: int = 0) -> dict:
    key = jax.random.key(seed)
    ks = jax.random.split(key, 20)
    nrm = jax.random.normal
    f32 = jnp.float32
    return {
        "x_prompt": nrm(ks[0], (BATCH, SEQ, D_MODEL), f32),
        "x_sample": nrm(ks[1], (DEC_BATCH, DEC_SEQ, D_MODEL), f32),
        "ln_in_g": 1.0 + 0.02 * nrm(ks[2], (D_MODEL,), f32),
        "ln_in_b": 0.02 * nrm(ks[3], (D_MODEL,), f32),
        "w_in": nrm(ks[4], (DEPTH, D_MODEL, IN_WIDTH), f32) * D_MODEL ** -0.5,
        "rpb": 0.1 * nrm(ks[5], (DEPTH, N_HEADS, 2 * KH_MAX - 1, 2 * KW - 1), f32),
        "w_att": nrm(ks[6], (DEPTH, ATT_WIDTH, D_MODEL), f32) * ATT_WIDTH ** -0.5,
        "w_four": nrm(ks[7], (DEPTH, F_WIDTH, D_MODEL), f32) * F_WIDTH ** -0.5,
        "w_gate": nrm(ks[8], (DEPTH, D_MODEL, 2 * D_MODEL), f32) * D_MODEL ** -0.5,
        "b_gate": 0.01 * nrm(ks[9], (DEPTH, 2 * D_MODEL), f32),
        "w_out": nrm(ks[10], (DEPTH, D_MODEL, D_MODEL), f32) * (D_MODEL ** -0.5 * DEEPNORM_BETA),
        "ln1_g": 1.0 + 0.02 * nrm(ks[11], (DEPTH, D_MODEL), f32),
        "ln1_b": 0.02 * nrm(ks[12], (DEPTH, D_MODEL), f32),
        "w_ffn_gate": nrm(ks[13], (DEPTH, D_MODEL, D_FF), f32) * D_MODEL ** -0.5,
        "w_ffn_up": nrm(ks[14], (DEPTH, D_MODEL, D_FF), f32) * D_MODEL ** -0.5,
        "w_ffn_down": nrm(ks[15], (DEPTH, D_FF, D_MODEL), f32) * (D_FF ** -0.5 * DEEPNORM_BETA),
        "ln2_g": 1.0 + 0.02 * nrm(ks[16], (DEPTH, D_MODEL), f32),
        "ln2_b": 0.02 * nrm(ks[17], (DEPTH, D_MODEL), f32),
    }


def reference(x_prompt, x_sample, ln_in_g, ln_in_b, w_in, rpb, w_att, w_four, w_gate,
              b_gate, w_out, ln1_g, ln1_b, w_ffn_gate, w_ffn_up, w_ffn_down, ln2_g, ln2_b):
    y_prompt = _trunk(x_prompt, ln_in_g, ln_in_b, w_in, rpb, w_att, w_four, w_gate, b_gate,
                      w_out, ln1_g, ln1_b, w_ffn_gate, w_ffn_up, w_ffn_down, ln2_g, ln2_b)
    y_sample = _trunk(x_sample, ln_in_g, ln_in_b, w_in, rpb, w_att, w_four, w_gate, b_gate,
                      w_out, ln1_g, ln1_b, w_ffn_gate, w_ffn_up, w_ffn_down, ln2_g, ln2_b)
    return (y_prompt, y_sample)
```

```python
import functools

import numpy as np
import jax
import jax.numpy as jnp
from jax import lax
from jax.experimental import pallas as pl
from jax.experimental.pallas import tpu as pltpu

F32 = jnp.float32
BF16 = jnp.bfloat16

GRID_W = 64
N_HEADS = 8
HEAD_DIM = 128
ATT_WIDTH = N_HEADS * HEAD_DIM
KH = 8
KW = 16
F_GROUPS = 4
F_GROUP_DIM = 256
F_WIDTH = F_GROUPS * F_GROUP_DIM
LN_EPS = 1e-5
MASK_VALUE = -1e30

V7X_VMEM_BYTES = 64 * 1024 * 1024
MIB = 1024 * 1024

FAST = 128
ROW_BLOCK = 8
HALO_ROWS = KH // 2


def _compiler_params(semantics, vmem_mib):
    assert vmem_mib * MIB <= V7X_VMEM_BYTES
    return pltpu.CompilerParams(dimension_semantics=semantics, vmem_limit_bytes=vmem_mib * MIB)


def _resident(block_shape, index_map):
    return pl.BlockSpec(block_shape, index_map, pipeline_mode=pl.Buffered(1))


def _layer_norm(x, g, b):
    mu = jnp.mean(x, axis=-1, keepdims=True)
    xc = x - mu
    var = jnp.mean(xc * xc, axis=-1, keepdims=True)
    return xc * lax.rsqrt(var + LN_EPS) * g + b


def _ln_kernel(x_ref, g_ref, b_ref, o_ref):
    o_ref[...] = _layer_norm(x_ref[...], g_ref[...], b_ref[...])


def _ln_call(x, g, b, tm=512):
    n, d = x.shape
    return pl.pallas_call(
        _ln_kernel,
        grid=(n // tm,),
        in_specs=[pl.BlockSpec((tm, d), lambda i: (i, 0)),
                  pl.BlockSpec((1, d), lambda i: (0, 0)),
                  pl.BlockSpec((1, d), lambda i: (0, 0))],
        out_specs=pl.BlockSpec((tm, d), lambda i: (i, 0)),
        out_shape=jax.ShapeDtypeStruct((n, d), F32),
        compiler_params=_compiler_params(("parallel",), 32),
        name="ln_in",
    )(x, g, b)


PROJ_TN = 512


def _proj_kernel(x_ref, win_ref, wg_ref, bg_ref, q_ref, k_ref, v_ref, u_ref, gate_ref, xb_ref):
    j = pl.program_id(1)

    @pl.when(j == 0)
    def _():
        xb_ref[...] = x_ref[...].astype(BF16)

    xb = xb_ref[...]
    p = jnp.dot(xb, win_ref[...], preferred_element_type=F32).astype(BF16)
    outs = (q_ref, k_ref, v_ref, u_ref)
    per_out = ATT_WIDTH // PROJ_TN
    for jj in range(len(outs) * per_out):
        @pl.when(j == jj)
        def _(jj=jj):
            lo = (jj % per_out) * PROJ_TN
            outs[jj // per_out][:, lo:lo + PROJ_TN] = p

    g = jnp.dot(xb, wg_ref[...], preferred_element_type=F32) + bg_ref[...]
    gate_ref[...] = jax.nn.sigmoid(g).astype(BF16)


def _proj_call(x, w_in, w_gate, b_gate, tm=512):
    n, d = x.shape
    tn = PROJ_TN
    assert w_in.shape == (d, 3 * ATT_WIDTH + F_WIDTH) and w_gate.shape == (d, 2 * d)
    assert w_in.shape[1] == w_gate.shape[1] and ATT_WIDTH == F_WIDTH
    nj = w_in.shape[1] // tn
    wide = pl.BlockSpec((tm, ATT_WIDTH), lambda i, j: (i, 0))
    return pl.pallas_call(
        _proj_kernel,
        grid=(n // tm, nj),
        in_specs=[pl.BlockSpec((tm, d), lambda i, j: (i, 0)),
                  pl.BlockSpec((d, tn), lambda i, j: (0, j)),
                  pl.BlockSpec((d, tn), lambda i, j: (0, j)),
                  pl.BlockSpec((1, tn), lambda i, j: (0, j))],
        out_specs=[wide, wide, wide, wide,
                   pl.BlockSpec((tm, tn), lambda i, j: (i, j))],
        out_shape=[jax.ShapeDtypeStruct((n, ATT_WIDTH), BF16)] * 4
                  + [jax.ShapeDtypeStruct((n, 2 * d), BF16)],
        scratch_shapes=[pltpu.VMEM((tm, d), BF16)],
        compiler_params=_compiler_params(("parallel", "arbitrary"), 48),
        name="proj",
    )(x, w_in, w_gate, b_gate)


def _attn_kernel(q_ref, kp_ref, kc_ref, kn_ref, vp_ref, vc_ref, vn_ref, bias_ref, o_ref,
                 kbuf, vbuf, *, nblk):
    jb = pl.program_id(1)
    halo = HALO_ROWS * GRID_W
    body_tok = ROW_BLOCK * GRID_W
    win_tok = KH * GRID_W
    kbuf[0:halo] = kp_ref[...]
    kbuf[halo:halo + body_tok] = kc_ref[...]
    kbuf[halo + body_tok:] = kn_ref[...]
    vbuf[0:halo] = vp_ref[...]
    vbuf[halo:halo + body_tok] = vc_ref[...]
    vbuf[halo + body_tok:] = vn_ref[...]

    lo = jnp.where(jb == 0, HALO_ROWS, 0)
    hi = jnp.where(jb == nblk - 1, HALO_ROWS, ROW_BLOCK - 1)

    c = lax.broadcasted_iota(jnp.int32, (GRID_W, win_tok), 0)
    kc = lax.broadcasted_iota(jnp.int32, (GRID_W, win_tok), 1) & (GRID_W - 1)
    cs = jnp.clip(c - KW // 2, 0, GRID_W - KW)
    rel = kc - cs
    valid = (rel >= 0) & (rel < KW)
    scale = HEAD_DIM ** -0.5

    def row_body(rr, carry):
        off = jnp.clip(rr, lo, hi)
        sidx = off - rr + (KH // 2 - 1)
        kstart = pl.multiple_of(off * GRID_W, GRID_W)
        qstart = pl.multiple_of(rr * GRID_W, GRID_W)
        for h in range(N_HEADS):
            hs = slice(h * HEAD_DIM, (h + 1) * HEAD_DIM)
            q = q_ref[pl.ds(qstart, GRID_W), hs]
            kw = kbuf[pl.ds(kstart, win_tok), hs]
            vw = vbuf[pl.ds(kstart, win_tok), hs]
            s = lax.dot_general(q, kw, (((1,), (1,)), ((), ())), preferred_element_type=F32)
            s = s * scale + bias_ref[sidx, h]
            s = jnp.where(valid, s, MASK_VALUE)
            m = jnp.max(s, axis=-1, keepdims=True)
            p = jnp.exp(s - m)
            l = jnp.sum(p, axis=-1, keepdims=True)
            o = jnp.dot(p.astype(BF16), vw, preferred_element_type=F32) / l
            o_ref[pl.ds(qstart, GRID_W), hs] = o.astype(BF16)
        return carry

    lax.fori_loop(0, ROW_BLOCK, row_body, 0)


def _bias_table(rpb):
    c = np.arange(GRID_W)[:, None]
    kc = np.arange(GRID_W)[None, :]
    idx = np.clip(kc - c + (KW - 1), 0, 2 * KW - 2)
    t = rpb[:, :, idx]
    tabs = jnp.stack([t[:, s:s + KH] for s in range(KH)], axis=0)
    return tabs.transpose(0, 1, 3, 2, 4).reshape(KH, N_HEADS, GRID_W, KH * GRID_W)


def _attn_call(q, k, v, bias, batch, seq):
    rows = seq // GRID_W
    assert rows % ROW_BLOCK == 0 and rows >= 2 * ROW_BLOCK
    nblk = rows // ROW_BLOCK
    body_tok = ROW_BLOCK * GRID_W
    halo = HALO_ROWS * GRID_W
    halo_per_body = body_tok // halo
    last_halo = batch * nblk * halo_per_body - 1

    def cur(b, j):
        return (b * nblk + j, 0)

    def prev(b, j):
        return (jnp.maximum((b * nblk + j) * halo_per_body - 1, 0), 0)

    def nxt(b, j):
        return (jnp.minimum((b * nblk + j + 1) * halo_per_body, last_halo), 0)

    body_spec = pl.BlockSpec((body_tok, ATT_WIDTH), cur)
    prev_spec = pl.BlockSpec((halo, ATT_WIDTH), prev)
    next_spec = pl.BlockSpec((halo, ATT_WIDTH), nxt)
    return pl.pallas_call(
        functools.partial(_attn_kernel, nblk=nblk),
        grid=(batch, nblk),
        in_specs=[body_spec, prev_spec, body_spec, next_spec, prev_spec, body_spec, next_spec,
                  _resident(bias.shape, lambda b, j: (0, 0, 0, 0))],
        out_specs=body_spec,
        out_shape=jax.ShapeDtypeStruct((batch * seq, ATT_WIDTH), BF16),
        scratch_shapes=[pltpu.VMEM((body_tok + 2 * halo, ATT_WIDTH), BF16)] * 2,
        compiler_params=_compiler_params(("parallel", "parallel"), 40),
        name="attn",
    )(q, k, k, k, v, v, v, bias)


FOUR1_TFB = 8


def _four1_kernel(u_ref, cs_ref, g_ref, y_ref, *, nj):
    for t in range(FOUR1_TFB):
        zr, zs = [], []
        for g in range(F_GROUPS):
            lo = t * F_WIDTH + g * F_GROUP_DIM
            zz = jnp.dot(u_ref[0, :, lo:lo + F_GROUP_DIM], cs_ref[...],
                         preferred_element_type=F32)
            zr.append(zz[:, :F_GROUP_DIM].astype(BF16))
            zs.append(zz[:, F_GROUP_DIM:].astype(BF16))
        z = jnp.concatenate([jnp.concatenate(zr, axis=1), jnp.concatenate(zs, axis=1)], axis=0)
        y = jnp.dot(g_ref[t], z, preferred_element_type=F32)
        y_ref[0, 0, :, t * F_WIDTH:(t + 1) * F_WIDTH] = y[:nj].astype(BF16)
        y_ref[0, 1, :, t * F_WIDTH:(t + 1) * F_WIDTH] = y[nj:].astype(BF16)


def _four2_kernel(y_ref, f_ref, o_ref, *, scale):
    y = y_ref[0].reshape(2 * ROW_BLOCK * FAST, F_WIDTH)
    r = jnp.dot(f_ref[...], y, preferred_element_type=F32) * scale
    o_ref[0] = r.reshape(FAST, ROW_BLOCK, F_WIDTH)


@functools.lru_cache(maxsize=None)
def _fourier_tables(seq):
    nj = seq // FAST
    cm = np.arange(F_GROUP_DIM, dtype=np.float64)
    ang = 2.0 * np.pi * np.outer(cm, cm) / F_GROUP_DIM
    chan = np.concatenate([np.cos(ang), np.sin(ang)], axis=1)

    tf = np.arange(FAST, dtype=np.float64)[:, None, None]
    klo = np.arange(nj, dtype=np.float64)[None, :, None]
    j = np.arange(nj, dtype=np.float64)[None, None, :]
    th = 2.0 * np.pi * ((klo * (FAST * j + tf)) % seq) / seq
    gc, gs = np.cos(th), np.sin(th)
    g = np.concatenate([np.concatenate([gc, -gs], axis=2),
                        np.concatenate([gs, gc], axis=2)], axis=1)

    khi = np.arange(FAST, dtype=np.float64)[:, None]
    tf2 = np.arange(FAST, dtype=np.float64)[None, :]
    ph = 2.0 * np.pi * ((khi * tf2) % FAST) / FAST
    eye = np.eye(ROW_BLOCK)
    fc = np.einsum("ht,lm->hlmt", np.cos(ph), eye).reshape(FAST * ROW_BLOCK, ROW_BLOCK * FAST)
    fs = np.einsum("ht,lm->hlmt", np.sin(ph), eye).reshape(FAST * ROW_BLOCK, ROW_BLOCK * FAST)
    f = np.concatenate([fc, -fs], axis=1)
    return chan.astype(np.float32), g.astype(np.float32), f.astype(np.float32)


def _fourier_call(u, batch, seq):
    nj = seq // FAST
    assert seq % FAST == 0 and nj % 16 == 0 and FAST % FOUR1_TFB == 0
    chan, g, f = (jnp.asarray(t).astype(BF16) for t in _fourier_tables(seq))
    lanes = FOUR1_TFB * F_WIDTH
    y = pl.pallas_call(
        functools.partial(_four1_kernel, nj=nj),
        grid=(batch, FAST // FOUR1_TFB),
        in_specs=[pl.BlockSpec((1, nj, lanes), lambda b, t: (b, 0, t)),
                  _resident(chan.shape, lambda b, t: (0, 0)),
                  pl.BlockSpec((FOUR1_TFB, 2 * nj, 2 * nj), lambda b, t: (t, 0, 0))],
        out_specs=pl.BlockSpec((1, 2, nj, lanes), lambda b, t: (b, 0, 0, t)),
        out_shape=jax.ShapeDtypeStruct((batch, 2, nj, FAST * F_WIDTH), BF16),
        compiler_params=_compiler_params(("parallel", "parallel"), 32),
        name="four1",
    )(u.reshape(batch, nj, FAST * F_WIDTH), chan, g)

    scale = float((seq * F_GROUP_DIM) ** -0.5)
    out = pl.pallas_call(
        functools.partial(_four2_kernel, scale=scale),
        grid=(batch, nj // ROW_BLOCK),
        in_specs=[pl.BlockSpec((1, 2, ROW_BLOCK, FAST, F_WIDTH), lambda b, k: (b, 0, k, 0, 0)),
                  _resident(f.shape, lambda b, k: (0, 0))],
        out_specs=pl.BlockSpec((1, FAST, ROW_BLOCK, F_WIDTH), lambda b, k: (b, 0, k, 0)),
        out_shape=jax.ShapeDtypeStruct((batch, FAST, nj, F_WIDTH), F32),
        compiler_params=_compiler_params(("parallel", "parallel"), 40),
        name="four2",
    )(y.reshape(batch, 2, nj, FAST, F_WIDTH), f)
    return out.reshape(batch * seq, F_WIDTH)


def _mix_kernel(a_ref, f_ref, gate_ref, x_ref, watt_ref, wfour_ref, wout_ref, g_ref, b_ref,
                o_ref, *, alpha):
    d = x_ref.shape[1]
    a = jnp.dot(a_ref[...], watt_ref[...], preferred_element_type=F32)
    f = jnp.dot(f_ref[...].astype(BF16), wfour_ref[...], preferred_element_type=F32)
    m = gate_ref[:, :d].astype(F32) * a + gate_ref[:, d:].astype(F32) * f
    y = jnp.dot(m.astype(BF16), wout_ref[...], preferred_element_type=F32)
    o_ref[...] = _layer_norm(alpha * x_ref[...] + y, g_ref[...], b_ref[...])


def _mix_call(a, f, gates, x, w_att, w_four, w_out, ln_g, ln_b, alpha, tm=256):
    n, d = x.shape
    row = lambda width: pl.BlockSpec((tm, width), lambda i: (i, 0))
    const = lambda shape: _resident(shape, lambda i: (0, 0))
    return pl.pallas_call(
        functools.partial(_mix_kernel, alpha=alpha),
        grid=(n // tm,),
        in_specs=[row(ATT_WIDTH), row(F_WIDTH), row(2 * d), row(d),
                  const(w_att.shape), const(w_four.shape), const(w_out.shape),
                  const((1, d)), const((1, d))],
        out_specs=row(d),
        out_shape=jax.ShapeDtypeStruct((n, d), F32),
        compiler_params=_compiler_params(("parallel",), 48),
        name="mix",
    )(a, f, gates, x, w_att, w_four, w_out, ln_g, ln_b)


def _ffn_kernel(x_ref, wg_ref, wu_ref, wd_ref, g_ref, b_ref, o_ref, xb_ref, acc_ref, *, alpha):
    k = pl.program_id(1)

    @pl.when(k == 0)
    def _():
        xb_ref[...] = x_ref[...].astype(BF16)
        acc_ref[...] = jnp.zeros_like(acc_ref)

    xb = xb_ref[...]
    gate = jnp.dot(xb, wg_ref[...], preferred_element_type=F32)
    up = jnp.dot(xb, wu_ref[...], preferred_element_type=F32)
    h = (jax.nn.silu(gate) * up).astype(BF16)
    acc_ref[...] += jnp.dot(h, wd_ref[...], preferred_element_type=F32)

    @pl.when(k == pl.num_programs(1) - 1)
    def _():
        o_ref[...] = _layer_norm(alpha * x_ref[...] + acc_ref[...], g_ref[...], b_ref[...])


def _ffn_call(x, w_gate, w_up, w_down, ln_g, ln_b, alpha, tm=512, tf=512):
    n, d = x.shape
    dff = w_gate.shape[1]
    assert dff % tf == 0
    return pl.pallas_call(
        functools.partial(_ffn_kernel, alpha=alpha),
        grid=(n // tm, dff // tf),
        in_specs=[pl.BlockSpec((tm, d), lambda i, k: (i, 0)),
                  pl.BlockSpec((d, tf), lambda i, k: (0, k)),
                  pl.BlockSpec((d, tf), lambda i, k: (0, k)),
                  pl.BlockSpec((tf, d), lambda i, k: (k, 0)),
                  pl.BlockSpec((1, d), lambda i, k: (0, 0)),
                  pl.BlockSpec((1, d), lambda i, k: (0, 0))],
        out_specs=pl.BlockSpec((tm, d), lambda i, k: (i, 0)),
        out_shape=jax.ShapeDtypeStruct((n, d), F32),
        scratch_shapes=[pltpu.VMEM((tm, d), BF16), pltpu.VMEM((tm, d), F32)],
        compiler_params=_compiler_params(("parallel", "arbitrary"), 48),
        name="ffn",
    )(x, w_gate, w_up, w_down, ln_g, ln_b)


def _trunk(x, layers, ln_in, alpha):
    batch, seq, d = x.shape
    h = _ln_call(x.reshape(batch * seq, d), *ln_in)
    for p in layers:
        q, k, v, u, gates = _proj_call(h, p["w_in"], p["w_gate"], p["b_gate"])
        a = _attn_call(q, k, v, p["bias"], batch, seq)
        f = _fourier_call(u, batch, seq)
        h = _mix_call(a, f, gates, h, p["w_att"], p["w_four"], p["w_out"],
                      p["ln1_g"], p["ln1_b"], alpha)
        h = _ffn_call(h, p["w_ffn_gate"], p["w_ffn_up"], p["w_ffn_down"],
                      p["ln2_g"], p["ln2_b"], alpha)
    return h.reshape(batch, seq, d)


def kernel(x_prompt, x_sample, ln_in_g, ln_in_b, w_in, rpb, w_att, w_four, w_gate, b_gate, w_out,
           ln1_g, ln1_b, w_ffn_gate, w_ffn_up, w_ffn_down, ln2_g, ln2_b):
    depth = w_in.shape[0]
    alpha = (2.0 * depth) ** 0.25
    row = lambda v: v.reshape(1, -1)
    layers = []
    for l in range(depth):
        layers.append(dict(
            w_in=w_in[l].astype(BF16), w_gate=w_gate[l].astype(BF16), b_gate=row(b_gate[l]),
            bias=_bias_table(rpb[l]),
            w_att=w_att[l].astype(BF16), w_four=w_four[l].astype(BF16), w_out=w_out[l].astype(BF16),
            ln1_g=row(ln1_g[l]), ln1_b=row(ln1_b[l]),
            w_ffn_gate=w_ffn_gate[l].astype(BF16), w_ffn_up=w_ffn_up[l].astype(BF16),
            w_ffn_down=w_ffn_down[l].astype(BF16),
            ln2_g=row(ln2_g[l]), ln2_b=row(ln2_b[l])))
    ln_in = (row(ln_in_g), row(ln_in_b))
    return (_trunk(x_prompt, layers, ln_in, alpha), _trunk(x_sample, layers, ln_in, alpha))
```

```python
import functools

import numpy as np
import jax
import jax.numpy as jnp
from jax import lax
from jax.experimental import pallas as pl
from jax.experimental.pallas import tpu as pltpu

F32 = jnp.float32
BF16 = jnp.bfloat16

GRID_W = 64
N_HEADS = 8
HEAD_DIM = 128
ATT_WIDTH = N_HEADS * HEAD_DIM
KH = 8
KW = 16
F_GROUPS = 4
F_GROUP_DIM = 256
F_WIDTH = F_GROUPS * F_GROUP_DIM
LN_EPS = 1e-5
MASK_VALUE = -1e30

V7X_VMEM_BYTES = 64 * 1024 * 1024
MIB = 1024 * 1024

FAST = 128
ROW_BLOCK = 8
HALO_ROWS = KH // 2


def _compiler_params(semantics, vmem_mib):
    assert vmem_mib * MIB <= V7X_VMEM_BYTES
    return pltpu.CompilerParams(dimension_semantics=semantics, vmem_limit_bytes=vmem_mib * MIB)


def _resident(block_shape, index_map):
    return pl.BlockSpec(block_shape, index_map, pipeline_mode=pl.Buffered(1))


def _layer_norm(x, g, b):
    mu = jnp.mean(x, axis=-1, keepdims=True)
    xc = x - mu
    var = jnp.mean(xc * xc, axis=-1, keepdims=True)
    return xc * lax.rsqrt(var + LN_EPS) * g + b


def _ln_kernel(x_ref, g_ref, b_ref, o_ref):
    o_ref[...] = _layer_norm(x_ref[...], g_ref[...], b_ref[...])


def _ln_call(x, g, b, tm=512):
    n, d = x.shape
    return pl.pallas_call(
        _ln_kernel,
        grid=(n // tm,),
        in_specs=[pl.BlockSpec((tm, d), lambda i: (i, 0)),
                  pl.BlockSpec((1, d), lambda i: (0, 0)),
                  pl.BlockSpec((1, d), lambda i: (0, 0))],
        out_specs=pl.BlockSpec((tm, d), lambda i: (i, 0)),
        out_shape=jax.ShapeDtypeStruct((n, d), F32),
        compiler_params=_compiler_params(("parallel",), 32),
        name="ln_in",
    )(x, g, b)


PROJ_TN = 512


MXU_COLS = 256


def _col_chunks(width):
    return [slice(c, c + MXU_COLS) for c in range(0, width, MXU_COLS)]


def _proj_kernel(x_ref, win_ref, wg_ref, bg_ref, qkvu_ref, gate_ref, xb_ref):
    @pl.when(pl.program_id(1) == 0)
    def _():
        xb_ref[...] = x_ref[...].astype(BF16)

    xb = xb_ref[...]
    for cs in _col_chunks(PROJ_TN):
        qkvu_ref[:, cs] = jnp.dot(xb, win_ref[:, cs], preferred_element_type=F32).astype(BF16)
    for cs in _col_chunks(PROJ_TN):
        g = jnp.dot(xb, wg_ref[:, cs], preferred_element_type=F32) + bg_ref[:, cs]
        gate_ref[:, cs] = jax.nn.sigmoid(g).astype(BF16)


def _proj_call(x, w_in, w_gate, b_gate, tm=512):
    n, d = x.shape
    tn = PROJ_TN
    assert w_in.shape == (d, 3 * ATT_WIDTH + F_WIDTH) and w_gate.shape == (d, 2 * d)
    assert w_in.shape[1] == w_gate.shape[1] and ATT_WIDTH == F_WIDTH
    nj = w_in.shape[1] // tn
    per_out = ATT_WIDTH // tn
    return pl.pallas_call(
        _proj_kernel,
        grid=(n // tm, nj),
        in_specs=[pl.BlockSpec((tm, d), lambda i, j: (i, 0)),
                  pl.BlockSpec((d, tn), lambda i, j: (0, j)),
                  pl.BlockSpec((d, tn), lambda i, j: (0, j)),
                  pl.BlockSpec((1, tn), lambda i, j: (0, j))],
        out_specs=[pl.BlockSpec((None, tm, tn), lambda i, j: (j // per_out, i, j % per_out)),
                   pl.BlockSpec((tm, tn), lambda i, j: (i, j))],
        out_shape=[jax.ShapeDtypeStruct((4, n, ATT_WIDTH), BF16),
                   jax.ShapeDtypeStruct((n, 2 * d), BF16)],
        scratch_shapes=[pltpu.VMEM((tm, d), BF16)],
        compiler_params=_compiler_params(("parallel", "arbitrary"), 48),
        name="proj",
    )(x, w_in, w_gate, b_gate)


def _attn_kernel(q_ref, kp_ref, kc_ref, kn_ref, vp_ref, vc_ref, vn_ref, bias_ref, o_ref,
                 kbuf, vbuf, s_even, s_odd, p_scr, *, nblk):
    jb = pl.program_id(1)
    halo = HALO_ROWS * GRID_W
    body_tok = ROW_BLOCK * GRID_W
    win_tok = KH * GRID_W
    kbuf[0:halo] = kp_ref[...]
    kbuf[halo:halo + body_tok] = kc_ref[...]
    kbuf[halo + body_tok:] = kn_ref[...]
    vbuf[0:halo] = vp_ref[...]
    vbuf[halo:halo + body_tok] = vc_ref[...]
    vbuf[halo + body_tok:] = vn_ref[...]

    lo = jnp.where(jb == 0, HALO_ROWS, 0)
    hi = jnp.where(jb == nblk - 1, HALO_ROWS, ROW_BLOCK - 1)

    c = lax.broadcasted_iota(jnp.int32, (GRID_W, win_tok), 0)
    kc = lax.broadcasted_iota(jnp.int32, (GRID_W, win_tok), 1) & (GRID_W - 1)
    cs = jnp.clip(c - KW // 2, 0, GRID_W - KW)
    rel = kc - cs
    valid = (rel >= 0) & (rel < KW)
    scale = HEAD_DIM ** -0.5

    heads = [slice(h * HEAD_DIM, (h + 1) * HEAD_DIM) for h in range(N_HEADS)]

    def window(rr):
        off = jnp.clip(rr, lo, hi)
        return (off - rr + (KH // 2 - 1), pl.multiple_of(off * GRID_W, GRID_W),
                pl.multiple_of(rr * GRID_W, GRID_W))

    def scores(rr, s_ref):
        _, kstart, qstart = window(rr)
        for h, hs in enumerate(heads):
            s_ref[h] = lax.dot_general(q_ref[pl.ds(qstart, GRID_W), hs],
                                       kbuf[pl.ds(kstart, win_tok), hs],
                                       (((1,), (1,)), ((), ())), preferred_element_type=F32)

    def attend(rr, s_ref):
        sidx, kstart, qstart = window(rr)
        inv = []
        for h in range(N_HEADS):
            s = s_ref[h] * scale + bias_ref[sidx, h]
            s = jnp.where(valid, s, MASK_VALUE)
            p = jnp.exp(s - jnp.max(s, axis=-1, keepdims=True))
            inv.append(1.0 / jnp.sum(p, axis=-1, keepdims=True))
            p_scr[h] = p.astype(BF16)
        for h, hs in enumerate(heads):
            o = jnp.dot(p_scr[h], vbuf[pl.ds(kstart, win_tok), hs], preferred_element_type=F32)
            o_ref[pl.ds(qstart, GRID_W), hs] = (o * inv[h]).astype(BF16)

    scores(0, s_even)

    def pair_body(i, carry):
        rr = 2 * i
        scores(rr + 1, s_odd)
        attend(rr, s_even)
        scores(jnp.minimum(rr + 2, ROW_BLOCK - 1), s_even)
        attend(rr + 1, s_odd)
        return carry

    lax.fori_loop(0, ROW_BLOCK // 2, pair_body, 0)


def _bias_table(rpb):
    c = np.arange(GRID_W)[:, None]
    kc = np.arange(GRID_W)[None, :]
    idx = np.clip(kc - c + (KW - 1), 0, 2 * KW - 2)
    t = rpb[:, :, idx]
    tabs = jnp.stack([t[:, s:s + KH] for s in range(KH)], axis=0)
    return tabs.transpose(0, 1, 3, 2, 4).reshape(KH, N_HEADS, GRID_W, KH * GRID_W)


def _attn_call(qkvu, bias, batch, seq):
    rows = seq // GRID_W
    assert rows % ROW_BLOCK == 0 and rows >= 2 * ROW_BLOCK
    nblk = rows // ROW_BLOCK
    body_tok = ROW_BLOCK * GRID_W
    halo = HALO_ROWS * GRID_W
    halo_per_body = body_tok // halo
    last_halo = batch * nblk * halo_per_body - 1

    def cur(plane):
        return pl.BlockSpec((None, body_tok, ATT_WIDTH), lambda b, j: (plane, b * nblk + j, 0))

    def prev(plane):
        return pl.BlockSpec(
            (None, halo, ATT_WIDTH),
            lambda b, j: (plane, jnp.maximum((b * nblk + j) * halo_per_body - 1, 0), 0))

    def nxt(plane):
        return pl.BlockSpec(
            (None, halo, ATT_WIDTH),
            lambda b, j: (plane, jnp.minimum((b * nblk + j + 1) * halo_per_body, last_halo), 0))

    return pl.pallas_call(
        functools.partial(_attn_kernel, nblk=nblk),
        grid=(batch, nblk),
        in_specs=[cur(0), prev(1), cur(1), nxt(1), prev(2), cur(2), nxt(2),
                  _resident(bias.shape, lambda b, j: (0, 0, 0, 0))],
        out_specs=pl.BlockSpec((body_tok, ATT_WIDTH), lambda b, j: (b * nblk + j, 0)),
        out_shape=jax.ShapeDtypeStruct((batch * seq, ATT_WIDTH), BF16),
        scratch_shapes=[pltpu.VMEM((body_tok + 2 * halo, ATT_WIDTH), BF16)] * 2
                       + [pltpu.VMEM((N_HEADS, GRID_W, KH * GRID_W), F32)] * 2
                       + [pltpu.VMEM((N_HEADS, GRID_W, KH * GRID_W), BF16)],
        compiler_params=_compiler_params(("parallel", "parallel"), 40),
        name="attn",
    )(*([qkvu] * 7), bias)


FOUR1_TFB = 8


def _four1_kernel(u_ref, cs_ref, g_ref, y_ref, *, nj):
    for t in range(FOUR1_TFB):
        zr, zs = [], []
        for g in range(F_GROUPS):
            lo = t * F_WIDTH + g * F_GROUP_DIM
            zz = jnp.dot(u_ref[0, :, lo:lo + F_GROUP_DIM], cs_ref[...],
                         preferred_element_type=F32)
            zr.append(zz[:, :F_GROUP_DIM].astype(BF16))
            zs.append(zz[:, F_GROUP_DIM:].astype(BF16))
        z = jnp.concatenate([jnp.concatenate(zr, axis=1), jnp.concatenate(zs, axis=1)], axis=0)
        y = jnp.dot(g_ref[t], z, preferred_element_type=F32)
        y_ref[0, 0, :, t * F_WIDTH:(t + 1) * F_WIDTH] = y[:nj].astype(BF16)
        y_ref[0, 1, :, t * F_WIDTH:(t + 1) * F_WIDTH] = y[nj:].astype(BF16)


def _four2_kernel(y_ref, f_ref, o_ref, *, scale):
    y = y_ref[0].reshape(2 * ROW_BLOCK * FAST, F_WIDTH)
    r = jnp.dot(f_ref[...], y, preferred_element_type=F32) * scale
    o_ref[0] = r.reshape(FAST, ROW_BLOCK, F_WIDTH)


@functools.lru_cache(maxsize=None)
def _fourier_tables(seq):
    nj = seq // FAST
    cm = np.arange(F_GROUP_DIM, dtype=np.float64)
    ang = 2.0 * np.pi * np.outer(cm, cm) / F_GROUP_DIM
    chan = np.concatenate([np.cos(ang), np.sin(ang)], axis=1)

    tf = np.arange(FAST, dtype=np.float64)[:, None, None]
    klo = np.arange(nj, dtype=np.float64)[None, :, None]
    j = np.arange(nj, dtype=np.float64)[None, None, :]
    th = 2.0 * np.pi * ((klo * (FAST * j + tf)) % seq) / seq
    gc, gs = np.cos(th), np.sin(th)
    g = np.concatenate([np.concatenate([gc, -gs], axis=2),
                        np.concatenate([gs, gc], axis=2)], axis=1)

    khi = np.arange(FAST, dtype=np.float64)[:, None]
    tf2 = np.arange(FAST, dtype=np.float64)[None, :]
    ph = 2.0 * np.pi * ((khi * tf2) % FAST) / FAST
    eye = np.eye(ROW_BLOCK)
    fc = np.einsum("ht,lm->hlmt", np.cos(ph), eye).reshape(FAST * ROW_BLOCK, ROW_BLOCK * FAST)
    fs = np.einsum("ht,lm->hlmt", np.sin(ph), eye).reshape(FAST * ROW_BLOCK, ROW_BLOCK * FAST)
    f = np.concatenate([fc, -fs], axis=1)
    return chan.astype(np.float32), g.astype(np.float32), f.astype(np.float32)


def _fourier_call(u, batch, seq):
    nj = seq // FAST
    assert seq % FAST == 0 and nj % 16 == 0 and FAST % FOUR1_TFB == 0
    chan, g, f = (jnp.asarray(t).astype(BF16) for t in _fourier_tables(seq))
    lanes = FOUR1_TFB * F_WIDTH
    y = pl.pallas_call(
        functools.partial(_four1_kernel, nj=nj),
        grid=(batch, FAST // FOUR1_TFB),
        in_specs=[pl.BlockSpec((1, nj, lanes), lambda b, t: (b, 0, t)),
                  _resident(chan.shape, lambda b, t: (0, 0)),
                  pl.BlockSpec((FOUR1_TFB, 2 * nj, 2 * nj), lambda b, t: (t, 0, 0))],
        out_specs=pl.BlockSpec((1, 2, nj, lanes), lambda b, t: (b, 0, 0, t)),
        out_shape=jax.ShapeDtypeStruct((batch, 2, nj, FAST * F_WIDTH), BF16),
        compiler_params=_compiler_params(("parallel", "parallel"), 32),
        name="four1",
    )(u.reshape(batch, nj, FAST * F_WIDTH), chan, g)

    scale = float((seq * F_GROUP_DIM) ** -0.5)
    out = pl.pallas_call(
        functools.partial(_four2_kernel, scale=scale),
        grid=(batch, nj // ROW_BLOCK),
        in_specs=[pl.BlockSpec((1, 2, ROW_BLOCK, FAST, F_WIDTH), lambda b, k: (b, 0, k, 0, 0)),
                  _resident(f.shape, lambda b, k: (0, 0))],
        out_specs=pl.BlockSpec((1, FAST, ROW_BLOCK, F_WIDTH), lambda b, k: (b, 0, k, 0)),
        out_shape=jax.ShapeDtypeStruct((batch, FAST, nj, F_WIDTH), F32),
        compiler_params=_compiler_params(("parallel", "parallel"), 40),
        name="four2",
    )(y.reshape(batch, 2, nj, FAST, F_WIDTH), f)
    return out.reshape(batch * seq, F_WIDTH)


def _mix_kernel(a_ref, f_ref, gate_ref, x_ref, watt_ref, wfour_ref, wout_ref, g_ref, b_ref,
                o_ref, *, alpha):
    d = x_ref.shape[1]
    a_in = a_ref[...]
    f_in = f_ref[...].astype(BF16)
    ms = []
    for cs in _col_chunks(d):
        a = jnp.dot(a_in, watt_ref[:, cs], preferred_element_type=F32)
        f = jnp.dot(f_in, wfour_ref[:, cs], preferred_element_type=F32)
        fs = slice(d + cs.start, d + cs.stop)
        ms.append((gate_ref[:, cs].astype(F32) * a + gate_ref[:, fs].astype(F32) * f).astype(BF16))
    y = jnp.dot(jnp.concatenate(ms, axis=1), wout_ref[...], preferred_element_type=F32)
    o_ref[...] = _layer_norm(alpha * x_ref[...] + y, g_ref[...], b_ref[...])


def _mix_call(a, f, gates, x, w_att, w_four, w_out, ln_g, ln_b, alpha, tm=256):
    n, d = x.shape
    row = lambda width: pl.BlockSpec((tm, width), lambda i: (i, 0))
    const = lambda shape: _resident(shape, lambda i: (0, 0))
    return pl.pallas_call(
        functools.partial(_mix_kernel, alpha=alpha),
        grid=(n // tm,),
        in_specs=[row(ATT_WIDTH), row(F_WIDTH), row(2 * d), row(d),
                  const(w_att.shape), const(w_four.shape), const(w_out.shape),
                  const((1, d)), const((1, d))],
        out_specs=row(d),
        out_shape=jax.ShapeDtypeStruct((n, d), F32),
        compiler_params=_compiler_params(("parallel",), 48),
        name="mix",
    )(a, f, gates, x, w_att, w_four, w_out, ln_g, ln_b)


FFN_LN_ROWS = 256


def _ffn_kernel(x_ref, wg_ref, wu_ref, wd_ref, g_ref, b_ref, o_ref, xb_ref, *, alpha):
    k = pl.program_id(1)

    @pl.when(k == 0)
    def _():
        xb_ref[...] = x_ref[...].astype(BF16)
        o_ref[...] = jnp.zeros_like(o_ref)

    xb = xb_ref[...]
    hs = []
    for cs in _col_chunks(wg_ref.shape[1]):
        gate = jnp.dot(xb, wg_ref[:, cs], preferred_element_type=F32)
        up = jnp.dot(xb, wu_ref[:, cs], preferred_element_type=F32)
        hs.append((jax.nn.silu(gate) * up).astype(BF16))
    h = jnp.concatenate(hs, axis=1)
    for cs in _col_chunks(o_ref.shape[1]):
        o_ref[:, cs] += jnp.dot(h, wd_ref[:, cs], preferred_element_type=F32)

    @pl.when(k == pl.num_programs(1) - 1)
    def _():
        for r in range(0, o_ref.shape[0], FFN_LN_ROWS):
            rs = slice(r, r + FFN_LN_ROWS)
            o_ref[rs] = _layer_norm(alpha * x_ref[rs] + o_ref[rs], g_ref[...], b_ref[...])


def _ffn_call(x, w_gate, w_up, w_down, ln_g, ln_b, alpha, tm=1024, tf=512):
    n, d = x.shape
    dff = w_gate.shape[1]
    assert dff % tf == 0
    return pl.pallas_call(
        functools.partial(_ffn_kernel, alpha=alpha),
        grid=(n // tm, dff // tf),
        in_specs=[pl.BlockSpec((tm, d), lambda i, k: (i, 0)),
                  pl.BlockSpec((d, tf), lambda i, k: (0, k)),
                  pl.BlockSpec((d, tf), lambda i, k: (0, k)),
                  pl.BlockSpec((tf, d), lambda i, k: (k, 0)),
                  pl.BlockSpec((1, d), lambda i, k: (0, 0)),
                  pl.BlockSpec((1, d), lambda i, k: (0, 0))],
        out_specs=pl.BlockSpec((tm, d), lambda i, k: (i, 0)),
        out_shape=jax.ShapeDtypeStruct((n, d), F32),
        scratch_shapes=[pltpu.VMEM((tm, d), BF16)],
        compiler_params=_compiler_params(("parallel", "arbitrary"), 60),
        name="ffn",
    )(x, w_gate, w_up, w_down, ln_g, ln_b)


def _trunk(x, layers, ln_in, alpha):
    batch, seq, d = x.shape
    h = _ln_call(x.reshape(batch * seq, d), *ln_in)
    for p in layers:
        qkvu, gates = _proj_call(h, p["w_in"], p["w_gate"], p["b_gate"])
        a = _attn_call(qkvu, p["bias"], batch, seq)
        f = _fourier_call(qkvu[3], batch, seq)
        h = _mix_call(a, f, gates, h, p["w_att"], p["w_four"], p["w_out"],
                      p["ln1_g"], p["ln1_b"], alpha)
        h = _ffn_call(h, p["w_ffn_gate"], p["w_ffn_up"], p["w_ffn_down"],
                      p["ln2_g"], p["ln2_b"], alpha)
    return h.reshape(batch, seq, d)


def kernel(x_prompt, x_sample, ln_in_g, ln_in_b, w_in, rpb, w_att, w_four, w_gate, b_gate, w_out,
           ln1_g, ln1_b, w_ffn_gate, w_ffn_up, w_ffn_down, ln2_g, ln2_b):
    depth = w_in.shape[0]
    alpha = (2.0 * depth) ** 0.25
    row = lambda v: v.reshape(1, -1)
    layers = []
    for l in range(depth):
        layers.append(dict(
            w_in=w_in[l].astype(BF16), w_gate=w_gate[l].astype(BF16), b_gate=row(b_gate[l]),
            bias=_bias_table(rpb[l]),
            w_att=w_att[l].astype(BF16), w_four=w_four[l].astype(BF16), w_out=w_out[l].astype(BF16),
            ln1_g=row(ln1_g[l]), ln1_b=row(ln1_b[l]),
            w_ffn_gate=w_ffn_gate[l].astype(BF16), w_ffn_up=w_ffn_up[l].astype(BF16),
            w_ffn_down=w_ffn_down[l].astype(BF16),
            ln2_g=row(ln2_g[l]), ln2_b=row(ln2_b[l])))
    ln_in = (row(ln_in_g), row(ln_in_b))
    return (_trunk(x_prompt, layers, ln_in, alpha), _trunk(x_sample, layers, ln_in, alpha))
```

```python
import functools

import numpy as np
import jax
import jax.numpy as jnp
from jax import lax
from jax.experimental import pallas as pl
from jax.experimental.pallas import tpu as pltpu

F32 = jnp.float32
BF16 = jnp.bfloat16

GRID_W = 64
N_HEADS = 8
HEAD_DIM = 128
ATT_WIDTH = N_HEADS * HEAD_DIM
KH = 8
KW = 16
F_GROUPS = 4
F_GROUP_DIM = 256
F_WIDTH = F_GROUPS * F_GROUP_DIM
LN_EPS = 1e-5
MASK_VALUE = -1e30
LOG2_E = 1.4426950408889634
Q_SCALE = HEAD_DIM ** -0.5 * LOG2_E

V7X_VMEM_BYTES = 64 * 1024 * 1024
MIB = 1024 * 1024

FAST = 128
ROW_BLOCK = 8
HALO_ROWS = KH // 2


def _compiler_params(semantics, vmem_mib):
    assert vmem_mib * MIB <= V7X_VMEM_BYTES
    return pltpu.CompilerParams(dimension_semantics=semantics, vmem_limit_bytes=vmem_mib * MIB)


def _resident(block_shape, index_map):
    return pl.BlockSpec(block_shape, index_map, pipeline_mode=pl.Buffered(1))


def _layer_norm(x, g, b):
    mu = jnp.mean(x, axis=-1, keepdims=True)
    xc = x - mu
    var = jnp.mean(xc * xc, axis=-1, keepdims=True)
    return xc * lax.rsqrt(var + LN_EPS) * g + b


MXU_COLS = 256
LN_ROWS = 256


def _col_chunks(width):
    return [slice(c, c + MXU_COLS) for c in range(0, width, MXU_COLS)]


def _row_chunks(nrows):
    return [slice(r, r + LN_ROWS) for r in range(0, nrows, LN_ROWS)]


def _proj_kernel(*refs, norm_input):
    if norm_input:
        x_ref, g_ref, b_ref, win_ref, wg_ref, bg_ref, qkvu_ref, gate_ref, xb_ref = refs
    else:
        x_ref, win_ref, wg_ref, bg_ref, qkvu_ref, gate_ref, xb_ref = refs

    @pl.when(pl.program_id(1) == 0)
    def _():
        for rs in _row_chunks(x_ref.shape[0]):
            x = x_ref[rs]
            if norm_input:
                x = _layer_norm(x, g_ref[...], b_ref[...])
            xb_ref[rs] = x.astype(BF16)

    xb = xb_ref[...]
    for cs in _col_chunks(wg_ref.shape[1]):
        g = jnp.dot(xb, wg_ref[:, cs], preferred_element_type=F32) + bg_ref[:, cs]
        gate_ref[:, cs] = jax.nn.sigmoid(g).astype(BF16)
    for cs in _col_chunks(win_ref.shape[1]):
        qkvu_ref[:, cs] = jnp.dot(xb, win_ref[:, cs], preferred_element_type=F32).astype(BF16)


def _proj_call(x, w_in, w_gate, b_gate, ln_in=None, tm=1024, tn=1024):
    n, d = x.shape
    assert w_in.shape == (d, 3 * ATT_WIDTH + F_WIDTH) and w_gate.shape == (d, 2 * d)
    assert w_in.shape[1] == w_gate.shape[1] and w_in.shape[1] % tn == 0 and n % tm == 0
    const = pl.BlockSpec((1, d), lambda i, j: (0, 0))
    cols = pl.BlockSpec((d, tn), lambda i, j: (0, j))
    tile = pl.BlockSpec((tm, tn), lambda i, j: (i, j))
    ln_specs, ln_args = ([const, const], list(ln_in)) if ln_in is not None else ([], [])
    return pl.pallas_call(
        functools.partial(_proj_kernel, norm_input=ln_in is not None),
        grid=(n // tm, w_in.shape[1] // tn),
        in_specs=[pl.BlockSpec((tm, d), lambda i, j: (i, 0))] + ln_specs
                 + [cols, cols, pl.BlockSpec((1, tn), lambda i, j: (0, j))],
        out_specs=[tile, tile],
        out_shape=[jax.ShapeDtypeStruct((n, w_in.shape[1]), BF16),
                   jax.ShapeDtypeStruct((n, 2 * d), BF16)],
        scratch_shapes=[pltpu.VMEM((tm, d), BF16)],
        compiler_params=_compiler_params(("parallel", "arbitrary"), 56),
        name="proj",
    )(x, *ln_args, w_in, w_gate, b_gate)


def _window_start_row(jb, rows):
    return jnp.clip(jb * ROW_BLOCK - HALO_ROWS, 0, rows - (ROW_BLOCK + 2 * HALO_ROWS))


def _attn_kernel(q_ref, k_ref, v_ref, bias_ref, o_ref, s_even, s_odd, p_scr, biasm, *, rows):
    jb = pl.program_id(1)
    win_tok = KH * GRID_W

    @pl.when((pl.program_id(0) == 0) & (jb == 0))
    def _():
        c = lax.broadcasted_iota(jnp.int32, (GRID_W, win_tok), 0)
        kc = lax.broadcasted_iota(jnp.int32, (GRID_W, win_tok), 1) & (GRID_W - 1)
        rel = kc - jnp.clip(c - KW // 2, 0, GRID_W - KW)
        valid = (rel >= 0) & (rel < KW)

        def mask_body(s, carry):
            for h in range(N_HEADS):
                biasm[s, h] = jnp.where(valid, bias_ref[s, h] * LOG2_E, MASK_VALUE)
            return carry

        lax.fori_loop(0, KH, mask_body, 0)

    heads = [slice(h * HEAD_DIM, (h + 1) * HEAD_DIM) for h in range(N_HEADS)]
    start_row = _window_start_row(jb, rows)

    def window(rr):
        r = jb * ROW_BLOCK + rr
        r_start = jnp.clip(r - KH // 2, 0, rows - KH)
        return (r_start - r + (KH - 1), pl.multiple_of((r_start - start_row) * GRID_W, GRID_W),
                pl.multiple_of(rr * GRID_W, GRID_W))

    def scores(rr, s_ref):
        _, kstart, qstart = window(rr)
        for h, hs in enumerate(heads):
            s_ref[h] = lax.dot_general(q_ref[pl.ds(qstart, GRID_W), hs],
                                       k_ref[pl.ds(kstart, win_tok), hs],
                                       (((1,), (1,)), ((), ())), preferred_element_type=F32)

    def attend(rr, s_ref):
        sidx, kstart, qstart = window(rr)
        for h, hs in enumerate(heads):
            s = s_ref[h] + biasm[sidx, h]
            p = jnp.exp2(s - jnp.max(s, axis=-1, keepdims=True))
            inv = 1.0 / jnp.sum(p, axis=-1, keepdims=True)
            p_scr[h] = p.astype(BF16)
            o = jnp.dot(p_scr[h], v_ref[pl.ds(kstart, win_tok), hs], preferred_element_type=F32)
            o_ref[pl.ds(qstart, GRID_W), hs] = (o * inv).astype(BF16)

    scores(0, s_even)

    def pair_body(i, carry):
        rr = 2 * i
        scores(rr + 1, s_odd)
        attend(rr, s_even)
        scores(jnp.minimum(rr + 2, ROW_BLOCK - 1), s_even)
        attend(rr + 1, s_odd)
        return carry

    lax.fori_loop(0, ROW_BLOCK // 2, pair_body, 0)


def _bias_table(rpb):
    c = np.arange(GRID_W)[None, :, None]
    kc = np.arange(GRID_W)[None, None, :]
    col_sel = (np.arange(2 * KW - 1)[:, None, None] == kc - c + (KW - 1)).astype(np.float32)
    s = np.arange(KH)[:, None, None]
    i = np.arange(KH)[None, :, None]
    row_sel = (np.arange(2 * KH - 1)[None, None, :] == s + i).astype(np.float32)
    tabs = jnp.einsum("hdm,sid,mck->shcik", rpb, row_sel, col_sel, precision=lax.Precision.HIGHEST)
    return tabs.reshape(KH, N_HEADS, GRID_W, KH * GRID_W)


def _attn_call(qkvu, bias, batch, seq):
    rows = seq // GRID_W
    fetch_rows = ROW_BLOCK + 2 * HALO_ROWS
    assert rows % ROW_BLOCK == 0 and rows >= fetch_rows
    body_tok = ROW_BLOCK * GRID_W
    score_shape = (N_HEADS, GRID_W, KH * GRID_W)

    def window(col_block):
        return pl.BlockSpec(
            (pl.Element(fetch_rows * GRID_W), pl.Element(ATT_WIDTH)),
            lambda b, j: ((b * rows + _window_start_row(j, rows)) * GRID_W, col_block * ATT_WIDTH))

    def body(b, j):
        return (b * (rows // ROW_BLOCK) + j, 0)

    return pl.pallas_call(
        functools.partial(_attn_kernel, rows=rows),
        grid=(batch, rows // ROW_BLOCK),
        in_specs=[pl.BlockSpec((body_tok, ATT_WIDTH), body), window(1), window(2),
                  _resident(bias.shape, lambda b, j: (0, 0, 0, 0))],
        out_specs=pl.BlockSpec((body_tok, ATT_WIDTH), body),
        out_shape=jax.ShapeDtypeStruct((batch * seq, ATT_WIDTH), BF16),
        scratch_shapes=[pltpu.VMEM(score_shape, F32), pltpu.VMEM(score_shape, F32),
                        pltpu.VMEM(score_shape, BF16), pltpu.VMEM(bias.shape, F32)],
        compiler_params=_compiler_params(("arbitrary", "arbitrary"), 48),
        name="attn",
    )(qkvu, qkvu, qkvu, bias)


FOUR1_TFB = 8


def _four1_kernel(u_ref, cs_ref, g_ref, y_ref, *, nj):
    for t in range(FOUR1_TFB):
        zr, zs = [], []
        for g in range(F_GROUPS):
            lo = t * F_WIDTH + g * F_GROUP_DIM
            zz = jnp.dot(u_ref[0, :, lo:lo + F_GROUP_DIM], cs_ref[...],
                         preferred_element_type=F32)
            zr.append(zz[:, :F_GROUP_DIM].astype(BF16))
            zs.append(zz[:, F_GROUP_DIM:].astype(BF16))
        z = jnp.concatenate([jnp.concatenate(zr, axis=1), jnp.concatenate(zs, axis=1)], axis=0)
        y = jnp.dot(g_ref[t], z, preferred_element_type=F32)
        y_ref[0, 0, :, t * F_WIDTH:(t + 1) * F_WIDTH] = y[:nj].astype(BF16)
        y_ref[0, 1, :, t * F_WIDTH:(t + 1) * F_WIDTH] = y[nj:].astype(BF16)


def _four2_kernel(y_ref, f_ref, o_ref, *, scale):
    y = y_ref[0].reshape(2 * ROW_BLOCK * FAST, F_WIDTH)
    r = jnp.dot(f_ref[...], y, preferred_element_type=F32) * scale
    o_ref[0] = r.reshape(FAST, ROW_BLOCK, F_WIDTH)


@functools.lru_cache(maxsize=None)
def _fourier_tables(seq):
    nj = seq // FAST
    cm = np.arange(F_GROUP_DIM, dtype=np.float64)
    ang = 2.0 * np.pi * np.outer(cm, cm) / F_GROUP_DIM
    chan = np.concatenate([np.cos(ang), np.sin(ang)], axis=1)

    tf = np.arange(FAST, dtype=np.float64)[:, None, None]
    klo = np.arange(nj, dtype=np.float64)[None, :, None]
    j = np.arange(nj, dtype=np.float64)[None, None, :]
    th = 2.0 * np.pi * ((klo * (FAST * j + tf)) % seq) / seq
    gc, gs = np.cos(th), np.sin(th)
    g = np.concatenate([np.concatenate([gc, -gs], axis=2),
                        np.concatenate([gs, gc], axis=2)], axis=1)

    khi = np.arange(FAST, dtype=np.float64)[:, None]
    tf2 = np.arange(FAST, dtype=np.float64)[None, :]
    ph = 2.0 * np.pi * ((khi * tf2) % FAST) / FAST
    eye = np.eye(ROW_BLOCK)
    fc = np.einsum("ht,lm->hlmt", np.cos(ph), eye).reshape(FAST * ROW_BLOCK, ROW_BLOCK * FAST)
    fs = np.einsum("ht,lm->hlmt", np.sin(ph), eye).reshape(FAST * ROW_BLOCK, ROW_BLOCK * FAST)
    f = np.concatenate([fc, -fs], axis=1)
    return chan.astype(np.float32), g.astype(np.float32), f.astype(np.float32)


def _fourier_call(u, batch, seq):
    nj = seq // FAST
    assert seq % FAST == 0 and nj % 16 == 0 and FAST % FOUR1_TFB == 0
    chan, g, f = (jnp.asarray(t).astype(BF16) for t in _fourier_tables(seq))
    lanes = FOUR1_TFB * F_WIDTH
    y = pl.pallas_call(
        functools.partial(_four1_kernel, nj=nj),
        grid=(batch, FAST // FOUR1_TFB),
        in_specs=[pl.BlockSpec((1, nj, lanes), lambda b, t: (b, 0, t)),
                  _resident(chan.shape, lambda b, t: (0, 0)),
                  pl.BlockSpec((FOUR1_TFB, 2 * nj, 2 * nj), lambda b, t: (t, 0, 0))],
        out_specs=pl.BlockSpec((1, 2, nj, lanes), lambda b, t: (b, 0, 0, t)),
        out_shape=jax.ShapeDtypeStruct((batch, 2, nj, FAST * F_WIDTH), BF16),
        compiler_params=_compiler_params(("parallel", "parallel"), 32),
        name="four1",
    )(u.reshape(batch, nj, FAST * F_WIDTH), chan, g)

    scale = float((seq * F_GROUP_DIM) ** -0.5)
    out = pl.pallas_call(
        functools.partial(_four2_kernel, scale=scale),
        grid=(batch, nj // ROW_BLOCK),
        in_specs=[pl.BlockSpec((1, 2, ROW_BLOCK, FAST, F_WIDTH), lambda b, k: (b, 0, k, 0, 0)),
                  _resident(f.shape, lambda b, k: (0, 0))],
        out_specs=pl.BlockSpec((1, FAST, ROW_BLOCK, F_WIDTH), lambda b, k: (b, 0, k, 0)),
        out_shape=jax.ShapeDtypeStruct((batch, FAST, nj, F_WIDTH), F32),
        compiler_params=_compiler_params(("parallel", "parallel"), 40),
        name="four2",
    )(y.reshape(batch, 2, nj, FAST, F_WIDTH), f)
    return out.reshape(batch * seq, F_WIDTH)


def _mix_kernel(*refs, alpha, norm_input):
    if norm_input:
        gi_ref, bi_ref, *refs = refs
    a_ref, f_ref, gate_ref, x_ref, watt_ref, wfour_ref, wout_ref, g_ref, b_ref, o_ref = refs
    d = x_ref.shape[1]
    for rs in _row_chunks(x_ref.shape[0]):
        a_in = a_ref[rs]
        f_in = f_ref[rs].astype(BF16)
        ms = []
        for cs in _col_chunks(d):
            a = jnp.dot(a_in, watt_ref[:, cs], preferred_element_type=F32)
            f = jnp.dot(f_in, wfour_ref[:, cs], preferred_element_type=F32)
            fs = slice(d + cs.start, d + cs.stop)
            ms.append((gate_ref[rs, cs].astype(F32) * a + gate_ref[rs, fs].astype(F32) * f).astype(BF16))
        y = jnp.dot(jnp.concatenate(ms, axis=1), wout_ref[...], preferred_element_type=F32)
        x = x_ref[rs]
        if norm_input:
            x = _layer_norm(x, gi_ref[...], bi_ref[...])
        o_ref[rs] = _layer_norm(alpha * x + y, g_ref[...], b_ref[...])


def _mix_call(a, f, gates, x, w_att, w_four, w_out, ln_g, ln_b, alpha, ln_in=None, tm=512):
    n, d = x.shape
    row = lambda width: pl.BlockSpec((tm, width), lambda i: (i, 0))
    const = lambda shape: _resident(shape, lambda i: (0, 0))
    ln_specs, ln_args = ([const((1, d))] * 2, list(ln_in)) if ln_in is not None else ([], [])
    return pl.pallas_call(
        functools.partial(_mix_kernel, alpha=alpha, norm_input=ln_in is not None),
        grid=(n // tm,),
        in_specs=ln_specs + [row(ATT_WIDTH), row(F_WIDTH), row(2 * d), row(d),
                             const(w_att.shape), const(w_four.shape), const(w_out.shape),
                             const((1, d)), const((1, d))],
        out_specs=row(d),
        out_shape=jax.ShapeDtypeStruct((n, d), F32),
        compiler_params=_compiler_params(("parallel",), 56),
        name="mix",
    )(*ln_args, a, f, gates, x, w_att, w_four, w_out, ln_g, ln_b)


def _ffn_kernel(x_ref, wg_ref, wu_ref, wd_ref, g_ref, b_ref, o_ref, xb_ref, *, alpha):
    k = pl.program_id(1)

    @pl.when(k == 0)
    def _():
        xb_ref[...] = x_ref[...].astype(BF16)
        o_ref[...] = jnp.zeros_like(o_ref)

    xb = xb_ref[...]
    hs = []
    for cs in _col_chunks(wg_ref.shape[1]):
        gate = jnp.dot(xb, wg_ref[:, cs], preferred_element_type=F32)
        up = jnp.dot(xb, wu_ref[:, cs], preferred_element_type=F32)
        hs.append((jax.nn.silu(gate) * up).astype(BF16))
    h = jnp.concatenate(hs, axis=1)
    for cs in _col_chunks(o_ref.shape[1]):
        o_ref[:, cs] += jnp.dot(h, wd_ref[:, cs], preferred_element_type=F32)

    @pl.when(k == pl.num_programs(1) - 1)
    def _():
        for rs in _row_chunks(o_ref.shape[0]):
            o_ref[rs] = _layer_norm(alpha * x_ref[rs] + o_ref[rs], g_ref[...], b_ref[...])


def _ffn_call(x, w_gate, w_up, w_down, ln_g, ln_b, alpha, tm=1024, tf=512):
    n, d = x.shape
    dff = w_gate.shape[1]
    assert dff % tf == 0
    return pl.pallas_call(
        functools.partial(_ffn_kernel, alpha=alpha),
        grid=(n // tm, dff // tf),
        in_specs=[pl.BlockSpec((tm, d), lambda i, k: (i, 0)),
                  pl.BlockSpec((d, tf), lambda i, k: (0, k)),
                  pl.BlockSpec((d, tf), lambda i, k: (0, k)),
                  pl.BlockSpec((tf, d), lambda i, k: (k, 0)),
                  pl.BlockSpec((1, d), lambda i, k: (0, 0)),
                  pl.BlockSpec((1, d), lambda i, k: (0, 0))],
        out_specs=pl.BlockSpec((tm, d), lambda i, k: (i, 0)),
        out_shape=jax.ShapeDtypeStruct((n, d), F32),
        scratch_shapes=[pltpu.VMEM((tm, d), BF16)],
        compiler_params=_compiler_params(("parallel", "arbitrary"), 60),
        name="ffn",
    )(x, w_gate, w_up, w_down, ln_g, ln_b)


def _trunk(x, layers, ln_in, alpha):
    batch, seq, d = x.shape
    h = x.reshape(batch * seq, d)
    for l, p in enumerate(layers):
        norm = ln_in if l == 0 else None
        qkvu, gates = _proj_call(h, p["w_in"], p["w_gate"], p["b_gate"], ln_in=norm)
        a = _attn_call(qkvu, p["bias"], batch, seq)
        f = _fourier_call(qkvu[:, 3 * ATT_WIDTH:], batch, seq)
        h = _mix_call(a, f, gates, h, p["w_att"], p["w_four"], p["w_out"],
                      p["ln1_g"], p["ln1_b"], alpha, ln_in=norm)
        h = _ffn_call(h, p["w_ffn_gate"], p["w_ffn_up"], p["w_ffn_down"],
                      p["ln2_g"], p["ln2_b"], alpha)
    return h.reshape(batch, seq, d)


def kernel(x_prompt, x_sample, ln_in_g, ln_in_b, w_in, rpb, w_att, w_four, w_gate, b_gate, w_out,
           ln1_g, ln1_b, w_ffn_gate, w_ffn_up, w_ffn_down, ln2_g, ln2_b):
    depth = w_in.shape[0]
    alpha = (2.0 * depth) ** 0.25
    row = lambda v: v.reshape(1, -1)
    in_col_scale = np.ones((1, w_in.shape[2]), np.float32)
    in_col_scale[:, :ATT_WIDTH] = Q_SCALE
    layers = []
    for l in range(depth):
        layers.append(dict(
            w_in=(w_in[l] * in_col_scale).astype(BF16),
            w_gate=w_gate[l].astype(BF16), b_gate=row(b_gate[l]),
            bias=_bias_table(rpb[l]),
            w_att=w_att[l].astype(BF16), w_four=w_four[l].astype(BF16), w_out=w_out[l].astype(BF16),
            ln1_g=row(ln1_g[l]), ln1_b=row(ln1_b[l]),
            w_ffn_gate=w_ffn_gate[l].astype(BF16), w_ffn_up=w_ffn_up[l].astype(BF16),
            w_ffn_down=w_ffn_down[l].astype(BF16),
            ln2_g=row(ln2_g[l]), ln2_b=row(ln2_b[l])))
    ln_in = (row(ln_in_g), row(ln_in_b))
    return (_trunk(x_prompt, layers, ln_in, alpha), _trunk(x_sample, layers, ln_in, alpha))
```

```python
import functools

import numpy as np
import jax
import jax.numpy as jnp
from jax import lax
from jax.experimental import pallas as pl
from jax.experimental.pallas import tpu as pltpu

F32 = jnp.float32
BF16 = jnp.bfloat16

GRID_W = 64
N_HEADS = 8
HEAD_DIM = 128
ATT_WIDTH = N_HEADS * HEAD_DIM
KH = 8
KW = 16
F_GROUPS = 4
F_GROUP_DIM = 256
F_WIDTH = F_GROUPS * F_GROUP_DIM
LN_EPS = 1e-5
MASK_VALUE = -1e30
LOG2_E = 1.4426950408889634
Q_SCALE = HEAD_DIM ** -0.5 * LOG2_E

V7X_VMEM_BYTES = 64 * 1024 * 1024
MIB = 1024 * 1024

FAST = 128
ROW_BLOCK = 8
HALO_ROWS = KH // 2


def _compiler_params(semantics, vmem_mib):
    assert vmem_mib * MIB <= V7X_VMEM_BYTES
    return pltpu.CompilerParams(dimension_semantics=semantics, vmem_limit_bytes=vmem_mib * MIB)


def _resident(block_shape, index_map):
    return pl.BlockSpec(block_shape, index_map, pipeline_mode=pl.Buffered(1))


def _layer_norm(x, g, b):
    mu = jnp.mean(x, axis=-1, keepdims=True)
    xc = x - mu
    var = jnp.mean(xc * xc, axis=-1, keepdims=True)
    return xc * lax.rsqrt(var + LN_EPS) * g + b


CAST_ROWS = 256


def _cast_kernel(*refs, scaled):
    if scaled:
        w_ref, s_ref, o_ref = refs
        o_ref[...] = (w_ref[...] * s_ref[...]).astype(BF16)
    else:
        w_ref, o_ref = refs
        o_ref[...] = w_ref[...].astype(BF16)


def _to_bf16(w, col_scale=None):
    layers, r, c = w.shape
    assert r % CAST_ROWS == 0
    tile = pl.BlockSpec((None, CAST_ROWS, c), lambda l, i: (l, i, 0))
    scale_specs, scale_args = (([pl.BlockSpec((1, c), lambda l, i: (0, 0))], [col_scale])
                               if col_scale is not None else ([], []))
    return pl.pallas_call(
        functools.partial(_cast_kernel, scaled=col_scale is not None),
        grid=(layers, r // CAST_ROWS),
        in_specs=[tile] + scale_specs,
        out_specs=tile,
        out_shape=jax.ShapeDtypeStruct(w.shape, BF16),
        compiler_params=_compiler_params(("parallel", "parallel"), 32),
        name="cast",
    )(w, *scale_args)


MXU_COLS = 256
LN_ROWS = 256


def _col_chunks(width):
    return [slice(c, c + MXU_COLS) for c in range(0, width, MXU_COLS)]


def _row_chunks(nrows):
    return [slice(r, r + LN_ROWS) for r in range(0, nrows, LN_ROWS)]


def _proj_kernel(*refs, norm_input):
    if norm_input:
        x_ref, g_ref, b_ref, win_ref, wg_ref, bg_ref, qkvu_ref, gate_ref, xb_ref = refs
    else:
        x_ref, win_ref, wg_ref, bg_ref, qkvu_ref, gate_ref, xb_ref = refs

    @pl.when(pl.program_id(1) == 0)
    def _():
        for rs in _row_chunks(x_ref.shape[0]):
            x = x_ref[rs]
            if norm_input:
                x = _layer_norm(x, g_ref[...], b_ref[...])
            xb_ref[rs] = x.astype(BF16)

    xb = xb_ref[...]
    for cs in _col_chunks(wg_ref.shape[1]):
        g = jnp.dot(xb, wg_ref[:, cs], preferred_element_type=F32) + bg_ref[:, cs]
        gate_ref[:, cs] = jax.nn.sigmoid(g).astype(BF16)
    for cs in _col_chunks(win_ref.shape[1]):
        qkvu_ref[:, cs] = jnp.dot(xb, win_ref[:, cs], preferred_element_type=F32).astype(BF16)


def _proj_call(x, layer, w_in, w_gate, b_gate, ln_in=None, tm=1024, tn=1024):
    n, d = x.shape
    width = w_in.shape[2]
    assert w_in.shape[1:] == (d, 3 * ATT_WIDTH + F_WIDTH) and w_gate.shape[1:] == (d, 2 * d)
    assert width == w_gate.shape[2] and width % tn == 0 and n % tm == 0
    const = pl.BlockSpec((1, d), lambda i, j: (0, 0))
    cols = pl.BlockSpec((None, d, tn), lambda i, j: (layer, 0, j))
    tile = pl.BlockSpec((tm, tn), lambda i, j: (i, j))
    ln_specs, ln_args = ([const, const], list(ln_in)) if ln_in is not None else ([], [])
    return pl.pallas_call(
        functools.partial(_proj_kernel, norm_input=ln_in is not None),
        grid=(n // tm, width // tn),
        in_specs=[pl.BlockSpec((tm, d), lambda i, j: (i, 0))] + ln_specs
                 + [cols, cols, pl.BlockSpec((1, tn), lambda i, j: (0, j))],
        out_specs=[tile, tile],
        out_shape=[jax.ShapeDtypeStruct((n, width), BF16),
                   jax.ShapeDtypeStruct((n, 2 * d), BF16)],
        scratch_shapes=[pltpu.VMEM((tm, d), BF16)],
        compiler_params=_compiler_params(("parallel", "arbitrary"), 56),
        name="proj",
    )(x, *ln_args, w_in, w_gate, b_gate)


def _window_start_row(jb, rows):
    return jnp.clip(jb * ROW_BLOCK - HALO_ROWS, 0, rows - (ROW_BLOCK + 2 * HALO_ROWS))


def _attn_kernel(q_ref, k_ref, v_ref, bias_ref, o_ref, s_even, s_odd, p_scr, biasm, *, rows):
    jb = pl.program_id(1)
    win_tok = KH * GRID_W

    @pl.when((pl.program_id(0) == 0) & (jb == 0))
    def _():
        c = lax.broadcasted_iota(jnp.int32, (GRID_W, win_tok), 0)
        kc = lax.broadcasted_iota(jnp.int32, (GRID_W, win_tok), 1) & (GRID_W - 1)
        rel = kc - jnp.clip(c - KW // 2, 0, GRID_W - KW)
        valid = (rel >= 0) & (rel < KW)

        def mask_body(s, carry):
            for h in range(N_HEADS):
                biasm[s, h] = jnp.where(valid, bias_ref[s, h] * LOG2_E, MASK_VALUE)
            return carry

        lax.fori_loop(0, KH, mask_body, 0)

    heads = [slice(h * HEAD_DIM, (h + 1) * HEAD_DIM) for h in range(N_HEADS)]
    start_row = _window_start_row(jb, rows)

    def window(rr):
        r = jb * ROW_BLOCK + rr
        r_start = jnp.clip(r - KH // 2, 0, rows - KH)
        return (r_start - r + (KH - 1), pl.multiple_of((r_start - start_row) * GRID_W, GRID_W),
                pl.multiple_of(rr * GRID_W, GRID_W))

    def scores(rr, s_ref):
        _, kstart, qstart = window(rr)
        for h, hs in enumerate(heads):
            s_ref[h] = lax.dot_general(q_ref[pl.ds(qstart, GRID_W), hs],
                                       k_ref[pl.ds(kstart, win_tok), hs],
                                       (((1,), (1,)), ((), ())), preferred_element_type=F32)

    def attend(rr, s_ref):
        sidx, kstart, qstart = window(rr)
        for h, hs in enumerate(heads):
            s = s_ref[h] + biasm[sidx, h]
            p = jnp.exp2(s - jnp.max(s, axis=-1, keepdims=True))
            inv = 1.0 / jnp.sum(p, axis=-1, keepdims=True)
            p_scr[h] = p.astype(BF16)
            o = jnp.dot(p_scr[h], v_ref[pl.ds(kstart, win_tok), hs], preferred_element_type=F32)
            o_ref[pl.ds(qstart, GRID_W), hs] = (o * inv).astype(BF16)

    scores(0, s_even)

    def pair_body(i, carry):
        rr = 2 * i
        scores(rr + 1, s_odd)
        attend(rr, s_even)
        scores(jnp.minimum(rr + 2, ROW_BLOCK - 1), s_even)
        attend(rr + 1, s_odd)
        return carry

    lax.fori_loop(0, ROW_BLOCK // 2, pair_body, 0, unroll=2)


def _bias_table(rpb):
    c = np.arange(GRID_W)[None, :, None]
    kc = np.arange(GRID_W)[None, None, :]
    col_sel = (np.arange(2 * KW - 1)[:, None, None] == kc - c + (KW - 1)).astype(np.float32)
    s = np.arange(KH)[:, None, None]
    i = np.arange(KH)[None, :, None]
    row_sel = (np.arange(2 * KH - 1)[None, None, :] == s + i).astype(np.float32)
    tabs = jnp.einsum("hdm,sid,mck->shcik", rpb, row_sel, col_sel, precision=lax.Precision.HIGHEST)
    return tabs.reshape(KH, N_HEADS, GRID_W, KH * GRID_W)


def _attn_call(qkvu, bias, batch, seq):
    rows = seq // GRID_W
    fetch_rows = ROW_BLOCK + 2 * HALO_ROWS
    assert rows % ROW_BLOCK == 0 and rows >= fetch_rows
    body_tok = ROW_BLOCK * GRID_W
    score_shape = (N_HEADS, GRID_W, KH * GRID_W)

    def window(col_block):
        return pl.BlockSpec(
            (pl.Element(fetch_rows * GRID_W), pl.Element(ATT_WIDTH)),
            lambda b, j: ((b * rows + _window_start_row(j, rows)) * GRID_W, col_block * ATT_WIDTH))

    def body(b, j):
        return (b * (rows // ROW_BLOCK) + j, 0)

    return pl.pallas_call(
        functools.partial(_attn_kernel, rows=rows),
        grid=(batch, rows // ROW_BLOCK),
        in_specs=[pl.BlockSpec((body_tok, ATT_WIDTH), body), window(1), window(2),
                  _resident(bias.shape, lambda b, j: (0, 0, 0, 0))],
        out_specs=pl.BlockSpec((body_tok, ATT_WIDTH), body),
        out_shape=jax.ShapeDtypeStruct((batch * seq, ATT_WIDTH), BF16),
        scratch_shapes=[pltpu.VMEM(score_shape, F32), pltpu.VMEM(score_shape, F32),
                        pltpu.VMEM(score_shape, BF16), pltpu.VMEM(bias.shape, F32)],
        compiler_params=_compiler_params(("arbitrary", "arbitrary"), 48),
        name="attn",
    )(qkvu, qkvu, qkvu, bias)


FOUR1_ROWS = 1024


def _four1_kernel(u_ref, cs_ref, g_ref, y_ref, *, nj):
    for t in range(g_ref.shape[0]):
        zr, zs = [], []
        for g in range(F_GROUPS):
            lo = t * F_WIDTH + g * F_GROUP_DIM
            zz = jnp.dot(u_ref[0, :, lo:lo + F_GROUP_DIM], cs_ref[...],
                         preferred_element_type=F32)
            zr.append(zz[:, :F_GROUP_DIM].astype(BF16))
            zs.append(zz[:, F_GROUP_DIM:].astype(BF16))
        z = jnp.concatenate([jnp.concatenate(zr, axis=1), jnp.concatenate(zs, axis=1)], axis=0)
        y = jnp.dot(g_ref[t], z, preferred_element_type=F32)
        y_ref[0, 0, :, t * F_WIDTH:(t + 1) * F_WIDTH] = y[:nj].astype(BF16)
        y_ref[0, 1, :, t * F_WIDTH:(t + 1) * F_WIDTH] = y[nj:].astype(BF16)


def _four2_kernel(y_ref, f_ref, o_ref, *, scale):
    y = y_ref[0].reshape(2 * ROW_BLOCK * FAST, F_WIDTH)
    r = jnp.dot(f_ref[...], y, preferred_element_type=F32) * scale
    o_ref[0] = r.reshape(FAST, ROW_BLOCK, F_WIDTH)


@functools.lru_cache(maxsize=None)
def _fourier_tables(seq):
    nj = seq // FAST
    cm = np.arange(F_GROUP_DIM, dtype=np.float64)
    ang = 2.0 * np.pi * np.outer(cm, cm) / F_GROUP_DIM
    chan = np.concatenate([np.cos(ang), np.sin(ang)], axis=1)

    tf = np.arange(FAST, dtype=np.float64)[:, None, None]
    klo = np.arange(nj, dtype=np.float64)[None, :, None]
    j = np.arange(nj, dtype=np.float64)[None, None, :]
    th = 2.0 * np.pi * ((klo * (FAST * j + tf)) % seq) / seq
    gc, gs = np.cos(th), np.sin(th)
    g = np.concatenate([np.concatenate([gc, -gs], axis=2),
                        np.concatenate([gs, gc], axis=2)], axis=1)

    khi = np.arange(FAST, dtype=np.float64)[:, None]
    tf2 = np.arange(FAST, dtype=np.float64)[None, :]
    ph = 2.0 * np.pi * ((khi * tf2) % FAST) / FAST
    eye = np.eye(ROW_BLOCK)
    fc = np.einsum("ht,lm->hlmt", np.cos(ph), eye).reshape(FAST * ROW_BLOCK, ROW_BLOCK * FAST)
    fs = np.einsum("ht,lm->hlmt", np.sin(ph), eye).reshape(FAST * ROW_BLOCK, ROW_BLOCK * FAST)
    f = np.concatenate([fc, -fs], axis=1)
    return chan.astype(np.float32), g.astype(np.float32), f.astype(np.float32)


def _fourier_call(u, batch, seq):
    nj = seq // FAST
    tfb = FOUR1_ROWS // nj
    assert seq % FAST == 0 and nj % 16 == 0 and FOUR1_ROWS % nj == 0 and FAST % tfb == 0
    chan, g, f = (jnp.asarray(t).astype(BF16) for t in _fourier_tables(seq))
    lanes = tfb * F_WIDTH
    y = pl.pallas_call(
        functools.partial(_four1_kernel, nj=nj),
        grid=(batch, FAST // tfb),
        in_specs=[pl.BlockSpec((1, nj, lanes), lambda b, t: (b, 0, t)),
                  _resident(chan.shape, lambda b, t: (0, 0)),
                  pl.BlockSpec((tfb, 2 * nj, 2 * nj), lambda b, t: (t, 0, 0))],
        out_specs=pl.BlockSpec((1, 2, nj, lanes), lambda b, t: (b, 0, 0, t)),
        out_shape=jax.ShapeDtypeStruct((batch, 2, nj, FAST * F_WIDTH), BF16),
        compiler_params=_compiler_params(("parallel", "parallel"), 32),
        name="four1",
    )(u.reshape(batch, nj, FAST * F_WIDTH), chan, g)

    scale = float((seq * F_GROUP_DIM) ** -0.5)
    out = pl.pallas_call(
        functools.partial(_four2_kernel, scale=scale),
        grid=(batch, nj // ROW_BLOCK),
        in_specs=[pl.BlockSpec((1, 2, ROW_BLOCK, FAST, F_WIDTH), lambda b, k: (b, 0, k, 0, 0)),
                  _resident(f.shape, lambda b, k: (0, 0))],
        out_specs=pl.BlockSpec((1, FAST, ROW_BLOCK, F_WIDTH), lambda b, k: (b, 0, k, 0)),
        out_shape=jax.ShapeDtypeStruct((batch, FAST, nj, F_WIDTH), F32),
        compiler_params=_compiler_params(("parallel", "parallel"), 40),
        name="four2",
    )(y.reshape(batch, 2, nj, FAST, F_WIDTH), f)
    return out.reshape(batch * seq, F_WIDTH)


def _mix_kernel(*refs, alpha, norm_input):
    if norm_input:
        gi_ref, bi_ref, *refs = refs
    a_ref, f_ref, gate_ref, x_ref, watt_ref, wfour_ref, wout_ref, g_ref, b_ref, o_ref = refs
    d = x_ref.shape[1]
    for rs in _row_chunks(x_ref.shape[0]):
        a_in = a_ref[rs]
        f_in = f_ref[rs].astype(BF16)
        ms = []
        for cs in _col_chunks(d):
            a = jnp.dot(a_in, watt_ref[:, cs], preferred_element_type=F32)
            f = jnp.dot(f_in, wfour_ref[:, cs], preferred_element_type=F32)
            fs = slice(d + cs.start, d + cs.stop)
            ms.append((gate_ref[rs, cs].astype(F32) * a + gate_ref[rs, fs].astype(F32) * f).astype(BF16))
        y = jnp.dot(jnp.concatenate(ms, axis=1), wout_ref[...], preferred_element_type=F32)
        x = x_ref[rs]
        if norm_input:
            x = _layer_norm(x, gi_ref[...], bi_ref[...])
        o_ref[rs] = _layer_norm(alpha * x + y, g_ref[...], b_ref[...])


def _mix_call(a, f, gates, x, layer, w_att, w_four, w_out, ln_g, ln_b, alpha, ln_in=None, tm=512):
    n, d = x.shape
    row = lambda width: pl.BlockSpec((tm, width), lambda i: (i, 0))
    const = lambda shape: _resident(shape, lambda i: (0, 0))
    weight = lambda w: _resident((None,) + w.shape[1:], lambda i: (layer, 0, 0))
    ln_specs, ln_args = ([const((1, d))] * 2, list(ln_in)) if ln_in is not None else ([], [])
    return pl.pallas_call(
        functools.partial(_mix_kernel, alpha=alpha, norm_input=ln_in is not None),
        grid=(n // tm,),
        in_specs=ln_specs + [row(ATT_WIDTH), row(F_WIDTH), row(2 * d), row(d),
                             weight(w_att), weight(w_four), weight(w_out),
                             const((1, d)), const((1, d))],
        out_specs=row(d),
        out_shape=jax.ShapeDtypeStruct((n, d), F32),
        compiler_params=_compiler_params(("parallel",), 56),
        name="mix",
    )(*ln_args, a, f, gates, x, w_att, w_four, w_out, ln_g, ln_b)


def _ffn_kernel(x_ref, wg_ref, wu_ref, wd_ref, g_ref, b_ref, o_ref, xb_ref, *, alpha):
    k = pl.program_id(1)

    @pl.when(k == 0)
    def _():
        xb_ref[...] = x_ref[...].astype(BF16)
        o_ref[...] = jnp.zeros_like(o_ref)

    def hidden(xb):
        hs = []
        for cs in _col_chunks(wg_ref.shape[1]):
            gate = jnp.dot(xb, wg_ref[:, cs], preferred_element_type=F32)
            up = jnp.dot(xb, wu_ref[:, cs], preferred_element_type=F32)
            hs.append((jax.nn.silu(gate) * up).astype(BF16))
        return jnp.concatenate(hs, axis=1)

    last = pl.num_programs(1) - 1

    @pl.when(k < last)
    def _():
        h = hidden(xb_ref[...])
        for cs in _col_chunks(o_ref.shape[1]):
            o_ref[:, cs] += jnp.dot(h, wd_ref[:, cs], preferred_element_type=F32)

    @pl.when(k == last)
    def _():
        for rs in _row_chunks(o_ref.shape[0]):
            y = o_ref[rs] + jnp.dot(hidden(xb_ref[rs]), wd_ref[...], preferred_element_type=F32)
            o_ref[rs] = _layer_norm(alpha * x_ref[rs] + y, g_ref[...], b_ref[...])


def _ffn_call(x, layer, w_gate, w_up, w_down, ln_g, ln_b, alpha, tm=1024, tf=512):
    n, d = x.shape
    dff = w_gate.shape[2]
    assert dff % tf == 0
    return pl.pallas_call(
        functools.partial(_ffn_kernel, alpha=alpha),
        grid=(n // tm, dff // tf),
        in_specs=[pl.BlockSpec((tm, d), lambda i, k: (i, 0)),
                  pl.BlockSpec((None, d, tf), lambda i, k: (layer, 0, k)),
                  pl.BlockSpec((None, d, tf), lambda i, k: (layer, 0, k)),
                  pl.BlockSpec((None, tf, d), lambda i, k: (layer, k, 0)),
                  pl.BlockSpec((1, d), lambda i, k: (0, 0)),
                  pl.BlockSpec((1, d), lambda i, k: (0, 0))],
        out_specs=pl.BlockSpec((tm, d), lambda i, k: (i, 0)),
        out_shape=jax.ShapeDtypeStruct((n, d), F32),
        scratch_shapes=[pltpu.VMEM((tm, d), BF16)],
        compiler_params=_compiler_params(("parallel", "arbitrary"), 60),
        name="ffn",
    )(x, w_gate, w_up, w_down, ln_g, ln_b)


def _trunk(x, w, rows, ln_in, alpha):
    batch, seq, d = x.shape
    h = x.reshape(batch * seq, d)
    for l, p in enumerate(rows):
        norm = ln_in if l == 0 else None
        qkvu, gates = _proj_call(h, l, w["in"], w["gate"], p["b_gate"], ln_in=norm)
        a = _attn_call(qkvu, p["bias"], batch, seq)
        f = _fourier_call(qkvu[:, 3 * ATT_WIDTH:], batch, seq)
        h = _mix_call(a, f, gates, h, l, w["att"], w["four"], w["out"],
                      p["ln1_g"], p["ln1_b"], alpha, ln_in=norm)
        h = _ffn_call(h, l, w["ffn_gate"], w["ffn_up"], w["ffn_down"], p["ln2_g"], p["ln2_b"], alpha)
    return h.reshape(batch, seq, d)


def kernel(x_prompt, x_sample, ln_in_g, ln_in_b, w_in, rpb, w_att, w_four, w_gate, b_gate, w_out,
           ln1_g, ln1_b, w_ffn_gate, w_ffn_up, w_ffn_down, ln2_g, ln2_b):
    depth = w_in.shape[0]
    alpha = (2.0 * depth) ** 0.25
    row = lambda v: v.reshape(1, -1)
    in_col_scale = np.ones((1, w_in.shape[2]), np.float32)
    in_col_scale[:, :ATT_WIDTH] = Q_SCALE
    w = {"in": _to_bf16(w_in, jnp.asarray(in_col_scale)), "gate": _to_bf16(w_gate),
         "att": _to_bf16(w_att), "four": _to_bf16(w_four), "out": _to_bf16(w_out),
         "ffn_gate": _to_bf16(w_ffn_gate), "ffn_up": _to_bf16(w_ffn_up),
         "ffn_down": _to_bf16(w_ffn_down)}
    rows = [dict(b_gate=row(b_gate[l]), bias=_bias_table(rpb[l]),
                 ln1_g=row(ln1_g[l]), ln1_b=row(ln1_b[l]), ln2_g=row(ln2_g[l]), ln2_b=row(ln2_b[l]))
            for l in range(depth)]
    ln_in = (row(ln_in_g), row(ln_in_b))
    return (_trunk(x_prompt, w, rows, ln_in, alpha), _trunk(x_sample, w, rows, ln_in, alpha))
```

```python
import functools

import numpy as np
import jax
import jax.numpy as jnp
from jax import lax
from jax.experimental import pallas as pl
from jax.experimental.pallas import tpu as pltpu

F32 = jnp.float32
BF16 = jnp.bfloat16

GRID_W = 64
N_HEADS = 8
HEAD_DIM = 128
ATT_WIDTH = N_HEADS * HEAD_DIM
KH = 8
KW = 16
F_GROUPS = 4
F_GROUP_DIM = 256
F_WIDTH = F_GROUPS * F_GROUP_DIM
LN_EPS = 1e-5
MASK_VALUE = -1e30
LOG2_E = 1.4426950408889634
Q_SCALE = HEAD_DIM ** -0.5 * LOG2_E

V7X_VMEM_BYTES = 64 * 1024 * 1024
MIB = 1024 * 1024

FAST = 128
ROW_BLOCK = 8
HALO_ROWS = KH // 2


def _compiler_params(semantics, vmem_mib):
    assert vmem_mib * MIB <= V7X_VMEM_BYTES
    return pltpu.CompilerParams(dimension_semantics=semantics, vmem_limit_bytes=vmem_mib * MIB)


def _resident(block_shape, index_map):
    return pl.BlockSpec(block_shape, index_map, pipeline_mode=pl.Buffered(1))


def _layer_norm(x, g, b):
    mu = jnp.mean(x, axis=-1, keepdims=True)
    xc = x - mu
    var = jnp.mean(xc * xc, axis=-1, keepdims=True)
    return xc * lax.rsqrt(var + LN_EPS) * g + b


CAST_ROWS = 256


def _cast_kernel(*refs, scaled):
    if scaled:
        w_ref, s_ref, o_ref = refs
        o_ref[...] = (w_ref[...] * s_ref[...]).astype(BF16)
    else:
        w_ref, o_ref = refs
        o_ref[...] = w_ref[...].astype(BF16)


def _to_bf16(w, col_scale=None):
    layers, r, c = w.shape
    assert r % CAST_ROWS == 0
    tile = pl.BlockSpec((None, CAST_ROWS, c), lambda l, i: (l, i, 0))
    scale_specs, scale_args = (([pl.BlockSpec((1, c), lambda l, i: (0, 0))], [col_scale])
                               if col_scale is not None else ([], []))
    return pl.pallas_call(
        functools.partial(_cast_kernel, scaled=col_scale is not None),
        grid=(layers, r // CAST_ROWS),
        in_specs=[tile] + scale_specs,
        out_specs=tile,
        out_shape=jax.ShapeDtypeStruct(w.shape, BF16),
        compiler_params=_compiler_params(("parallel", "parallel"), 32),
        name="cast",
    )(w, *scale_args)


MXU_COLS = 256
LN_ROWS = 256


def _col_chunks(width):
    return [slice(c, c + MXU_COLS) for c in range(0, width, MXU_COLS)]


def _row_chunks(nrows):
    return [slice(r, r + LN_ROWS) for r in range(0, nrows, LN_ROWS)]


def _proj_kernel(*refs, norm_input):
    if norm_input:
        x_ref, g_ref, b_ref, win_ref, wg_ref, bg_ref, qkv_ref, u_ref, gate_ref, xb_ref = refs
    else:
        x_ref, win_ref, wg_ref, bg_ref, qkv_ref, u_ref, gate_ref, xb_ref = refs
    first = pl.program_id(1) == 0

    @pl.when(first)
    def _():
        for rs in _row_chunks(x_ref.shape[0]):
            x = x_ref[rs]
            if norm_input:
                x = _layer_norm(x, g_ref[...], b_ref[...])
            xb_ref[rs] = x.astype(BF16)

    xb = xb_ref[...]
    for cs in _col_chunks(wg_ref.shape[1]):
        g = jnp.dot(xb, wg_ref[:, cs], preferred_element_type=F32) + bg_ref[:, cs]
        gate_ref[:, cs] = jax.nn.sigmoid(g).astype(BF16)
    for cs in _col_chunks(win_ref.shape[1]):
        qkv_ref[:, cs] = jnp.dot(xb, win_ref[:, cs], preferred_element_type=F32).astype(BF16)

    @pl.when(first)
    def _():
        u_ref[...] = qkv_ref[...]


def _proj_call(x, layer, w_in, w_gate, b_gate, ln_in=None, tm=1024, tn=1024):
    n, d = x.shape
    width = w_in.shape[2]
    steps = width // tn
    assert w_in.shape[1:] == (d, 3 * ATT_WIDTH + F_WIDTH) and w_gate.shape[1:] == (d, 2 * d)
    assert width == w_gate.shape[2] and width % tn == 0 and n % tm == 0 and tn == F_WIDTH
    const = pl.BlockSpec((1, d), lambda i, j: (0, 0))
    tile = pl.BlockSpec((tm, tn), lambda i, j: (i, j))
    ln_specs, ln_args = ([const, const], list(ln_in)) if ln_in is not None else ([], [])
    return pl.pallas_call(
        functools.partial(_proj_kernel, norm_input=ln_in is not None),
        grid=(n // tm, steps),
        in_specs=[pl.BlockSpec((tm, d), lambda i, j: (i, 0))] + ln_specs
                 + [pl.BlockSpec((None, d, tn), lambda i, j: (layer, 0, (j + steps - 1) % steps)),
                    pl.BlockSpec((None, d, tn), lambda i, j: (layer, 0, j)),
                    pl.BlockSpec((1, tn), lambda i, j: (0, j))],
        out_specs=[pl.BlockSpec((tm, tn), lambda i, j: (i, jnp.maximum(j - 1, 0))),
                   pl.BlockSpec((tm, tn), lambda i, j: (i, 0)),
                   tile],
        out_shape=[jax.ShapeDtypeStruct((n, width - tn), BF16),
                   jax.ShapeDtypeStruct((n, tn), BF16),
                   jax.ShapeDtypeStruct((n, 2 * d), BF16)],
        scratch_shapes=[pltpu.VMEM((tm, d), BF16)],
        compiler_params=_compiler_params(("parallel", "arbitrary"), 60),
        name="proj",
    )(x, *ln_args, w_in, w_gate, b_gate)


def _window_start_row(jb, rows):
    return jnp.clip(jb * ROW_BLOCK - HALO_ROWS, 0, rows - (ROW_BLOCK + 2 * HALO_ROWS))


def _attn_kernel(q_ref, k_ref, v_ref, bias_ref, o_ref, s_even, s_odd, p_scr, biasm, *, rows):
    jb = pl.program_id(1)
    win_tok = KH * GRID_W

    @pl.when((pl.program_id(0) == 0) & (jb == 0))
    def _():
        c = lax.broadcasted_iota(jnp.int32, (GRID_W, win_tok), 0)
        kc = lax.broadcasted_iota(jnp.int32, (GRID_W, win_tok), 1) & (GRID_W - 1)
        rel = kc - jnp.clip(c - KW // 2, 0, GRID_W - KW)
        valid = (rel >= 0) & (rel < KW)

        def mask_body(s, carry):
            for h in range(N_HEADS):
                biasm[s, h] = jnp.where(valid, bias_ref[s, h] * LOG2_E, MASK_VALUE)
            return carry

        lax.fori_loop(0, KH, mask_body, 0)

    heads = [slice(h * HEAD_DIM, (h + 1) * HEAD_DIM) for h in range(N_HEADS)]
    start_row = _window_start_row(jb, rows)

    def window(rr):
        r = jb * ROW_BLOCK + rr
        r_start = jnp.clip(r - KH // 2, 0, rows - KH)
        return (r_start - r + (KH - 1), pl.multiple_of((r_start - start_row) * GRID_W, GRID_W),
                pl.multiple_of(rr * GRID_W, GRID_W))

    def scores(rr, s_ref):
        _, kstart, qstart = window(rr)
        for h, hs in enumerate(heads):
            s_ref[h] = lax.dot_general(q_ref[pl.ds(qstart, GRID_W), hs],
                                       k_ref[pl.ds(kstart, win_tok), hs],
                                       (((1,), (1,)), ((), ())), preferred_element_type=F32)

    def attend(rr, s_ref):
        sidx, kstart, qstart = window(rr)
        for h, hs in enumerate(heads):
            s = s_ref[h] + biasm[sidx, h]
            p = jnp.exp2(s - jnp.max(s, axis=-1, keepdims=True))
            inv = 1.0 / jnp.sum(p, axis=-1, keepdims=True)
            p_scr[h] = p.astype(BF16)
            o = jnp.dot(p_scr[h], v_ref[pl.ds(kstart, win_tok), hs], preferred_element_type=F32)
            o_ref[pl.ds(qstart, GRID_W), hs] = (o * inv).astype(BF16)

    scores(0, s_even)

    def pair_body(i, carry):
        rr = 2 * i
        scores(rr + 1, s_odd)
        attend(rr, s_even)
        scores(jnp.minimum(rr + 2, ROW_BLOCK - 1), s_even)
        attend(rr + 1, s_odd)
        return carry

    lax.fori_loop(0, ROW_BLOCK // 2, pair_body, 0, unroll=2)


def _bias_table(rpb):
    c = np.arange(GRID_W)[None, :, None]
    kc = np.arange(GRID_W)[None, None, :]
    col_sel = (np.arange(2 * KW - 1)[:, None, None] == kc - c + (KW - 1)).astype(np.float32)
    s = np.arange(KH)[:, None, None]
    i = np.arange(KH)[None, :, None]
    row_sel = (np.arange(2 * KH - 1)[None, None, :] == s + i).astype(np.float32)
    tabs = jnp.einsum("hdm,sid,mck->shcik", rpb, row_sel, col_sel, precision=lax.Precision.HIGHEST)
    return tabs.reshape(KH, N_HEADS, GRID_W, KH * GRID_W)


def _attn_call(qkv, bias, batch, seq):
    rows = seq // GRID_W
    fetch_rows = ROW_BLOCK + 2 * HALO_ROWS
    assert rows % ROW_BLOCK == 0 and rows >= fetch_rows
    body_tok = ROW_BLOCK * GRID_W
    score_shape = (N_HEADS, GRID_W, KH * GRID_W)

    def window(col_block):
        return pl.BlockSpec(
            (pl.Element(fetch_rows * GRID_W), pl.Element(ATT_WIDTH)),
            lambda b, j: ((b * rows + _window_start_row(j, rows)) * GRID_W, col_block * ATT_WIDTH))

    def body(b, j):
        return (b * (rows // ROW_BLOCK) + j, 0)

    return pl.pallas_call(
        functools.partial(_attn_kernel, rows=rows),
        grid=(batch, rows // ROW_BLOCK),
        in_specs=[pl.BlockSpec((body_tok, ATT_WIDTH), body), window(1), window(2),
                  _resident(bias.shape, lambda b, j: (0, 0, 0, 0))],
        out_specs=pl.BlockSpec((body_tok, ATT_WIDTH), body),
        out_shape=jax.ShapeDtypeStruct((batch * seq, ATT_WIDTH), BF16),
        scratch_shapes=[pltpu.VMEM(score_shape, F32), pltpu.VMEM(score_shape, F32),
                        pltpu.VMEM(score_shape, BF16), pltpu.VMEM(bias.shape, F32)],
        compiler_params=_compiler_params(("arbitrary", "arbitrary"), 48),
        name="attn",
    )(qkv, qkv, qkv, bias)


FOUR1_ROWS = 1024


def _four1_kernel(u_ref, cs_ref, g_ref, y_ref, *, nj):
    tfb = g_ref.shape[0]
    zz = []
    for g in range(F_GROUPS):
        ug = jnp.concatenate(
            [u_ref[0, :, t * F_WIDTH + g * F_GROUP_DIM:t * F_WIDTH + (g + 1) * F_GROUP_DIM]
             for t in range(tfb)], axis=0)
        zz.append(jnp.dot(ug, cs_ref[...], preferred_element_type=F32).astype(BF16))
    for t in range(tfb):
        rows = slice(t * nj, (t + 1) * nj)
        z = jnp.concatenate([jnp.concatenate([zg[rows, :F_GROUP_DIM] for zg in zz], axis=1),
                             jnp.concatenate([zg[rows, F_GROUP_DIM:] for zg in zz], axis=1)], axis=0)
        y = jnp.dot(g_ref[t], z, preferred_element_type=F32)
        y_ref[0, 0, :, t * F_WIDTH:(t + 1) * F_WIDTH] = y[:nj].astype(BF16)
        y_ref[0, 1, :, t * F_WIDTH:(t + 1) * F_WIDTH] = y[nj:].astype(BF16)


def _four2_kernel(y_ref, f_ref, p_ref, o_ref, *, scale):
    rs = []
    for l in range(ROW_BLOCK):
        yl = jnp.concatenate([y_ref[0, 0, l], y_ref[0, 1, l]], axis=0)
        rs.append((jnp.dot(f_ref[...], yl, preferred_element_type=F32) * scale).astype(BF16))
    r = jnp.dot(p_ref[...], jnp.concatenate(rs, axis=0), preferred_element_type=F32)
    o_ref[0] = r.reshape(FAST, ROW_BLOCK, F_WIDTH)


@functools.lru_cache(maxsize=None)
def _fourier_tables(seq):
    nj = seq // FAST
    cm = np.arange(F_GROUP_DIM, dtype=np.float64)
    ang = 2.0 * np.pi * np.outer(cm, cm) / F_GROUP_DIM
    chan = np.concatenate([np.cos(ang), np.sin(ang)], axis=1)

    tf = np.arange(FAST, dtype=np.float64)[:, None, None]
    klo = np.arange(nj, dtype=np.float64)[None, :, None]
    j = np.arange(nj, dtype=np.float64)[None, None, :]
    th = 2.0 * np.pi * ((klo * (FAST * j + tf)) % seq) / seq
    gc, gs = np.cos(th), np.sin(th)
    g = np.concatenate([np.concatenate([gc, -gs], axis=2),
                        np.concatenate([gs, gc], axis=2)], axis=1)

    khi = np.arange(FAST, dtype=np.float64)[:, None]
    tf2 = np.arange(FAST, dtype=np.float64)[None, :]
    ph = 2.0 * np.pi * ((khi * tf2) % FAST) / FAST
    f = np.concatenate([np.cos(ph), -np.sin(ph)], axis=1)
    place = np.einsum("hg,lm->hlmg", np.eye(FAST), np.eye(ROW_BLOCK)).reshape(
        FAST * ROW_BLOCK, ROW_BLOCK * FAST)
    return tuple(t.astype(np.float32) for t in (chan, g, f, place))


def _fourier_call(u, batch, seq):
    nj = seq // FAST
    tfb = FOUR1_ROWS // nj
    assert seq % FAST == 0 and nj % 16 == 0 and FOUR1_ROWS % nj == 0 and FAST % tfb == 0
    chan, g, f, place = (jnp.asarray(t).astype(BF16) for t in _fourier_tables(seq))
    lanes = tfb * F_WIDTH
    y = pl.pallas_call(
        functools.partial(_four1_kernel, nj=nj),
        grid=(batch, FAST // tfb),
        in_specs=[pl.BlockSpec((1, nj, lanes), lambda b, t: (b, 0, t)),
                  _resident(chan.shape, lambda b, t: (0, 0)),
                  pl.BlockSpec((tfb, 2 * nj, 2 * nj), lambda b, t: (t, 0, 0))],
        out_specs=pl.BlockSpec((1, 2, nj, lanes), lambda b, t: (b, 0, 0, t)),
        out_shape=jax.ShapeDtypeStruct((batch, 2, nj, FAST * F_WIDTH), BF16),
        compiler_params=_compiler_params(("parallel", "parallel"), 32),
        name="four1",
    )(u.reshape(batch, nj, FAST * F_WIDTH), chan, g)

    scale = float((seq * F_GROUP_DIM) ** -0.5)
    out = pl.pallas_call(
        functools.partial(_four2_kernel, scale=scale),
        grid=(batch, nj // ROW_BLOCK),
        in_specs=[pl.BlockSpec((1, 2, ROW_BLOCK, FAST, F_WIDTH), lambda b, k: (b, 0, k, 0, 0)),
                  _resident(f.shape, lambda b, k: (0, 0)),
                  _resident(place.shape, lambda b, k: (0, 0))],
        out_specs=pl.BlockSpec((1, FAST, ROW_BLOCK, F_WIDTH), lambda b, k: (b, 0, k, 0)),
        out_shape=jax.ShapeDtypeStruct((batch, FAST, nj, F_WIDTH), F32),
        compiler_params=_compiler_params(("parallel", "parallel"), 40),
        name="four2",
    )(y.reshape(batch, 2, nj, FAST, F_WIDTH), f, place)
    return out.reshape(batch * seq, F_WIDTH)


def _mix_kernel(*refs, alpha, norm_input):
    if norm_input:
        gi_ref, bi_ref, *refs = refs
    a_ref, f_ref, gate_ref, x_ref, watt_ref, wfour_ref, wout_ref, g_ref, b_ref, o_ref = refs
    d = x_ref.shape[1]
    for rs in _row_chunks(x_ref.shape[0]):
        a_in = a_ref[rs]
        f_in = f_ref[rs].astype(BF16)
        ms = []
        for cs in _col_chunks(d):
            a = jnp.dot(a_in, watt_ref[:, cs], preferred_element_type=F32)
            f = jnp.dot(f_in, wfour_ref[:, cs], preferred_element_type=F32)
            fs = slice(d + cs.start, d + cs.stop)
            ms.append((gate_ref[rs, cs].astype(F32) * a + gate_ref[rs, fs].astype(F32) * f).astype(BF16))
        y = jnp.dot(jnp.concatenate(ms, axis=1), wout_ref[...], preferred_element_type=F32)
        x = x_ref[rs]
        if norm_input:
            x = _layer_norm(x, gi_ref[...], bi_ref[...])
        o_ref[rs] = _layer_norm(alpha * x + y, g_ref[...], b_ref[...])


def _mix_call(a, f, gates, x, layer, w_att, w_four, w_out, ln_g, ln_b, alpha, ln_in=None, tm=512):
    n, d = x.shape
    row = lambda width: pl.BlockSpec((tm, width), lambda i: (i, 0))
    const = lambda shape: _resident(shape, lambda i: (0, 0))
    weight = lambda w: _resident((None,) + w.shape[1:], lambda i: (layer, 0, 0))
    ln_specs, ln_args = ([const((1, d))] * 2, list(ln_in)) if ln_in is not None else ([], [])
    return pl.pallas_call(
        functools.partial(_mix_kernel, alpha=alpha, norm_input=ln_in is not None),
        grid=(n // tm,),
        in_specs=ln_specs + [row(ATT_WIDTH), row(F_WIDTH), row(2 * d), row(d),
                             weight(w_att), weight(w_four), weight(w_out),
                             const((1, d)), const((1, d))],
        out_specs=row(d),
        out_shape=jax.ShapeDtypeStruct((n, d), F32),
        compiler_params=_compiler_params(("parallel",), 56),
        name="mix",
    )(*ln_args, a, f, gates, x, w_att, w_four, w_out, ln_g, ln_b)


def _ffn_kernel(x_ref, wg_ref, wu_ref, wd_ref, g_ref, b_ref, o_ref, xb_ref, *, alpha):
    k = pl.program_id(1)

    @pl.when(k == 0)
    def _():
        xb_ref[...] = x_ref[...].astype(BF16)
        o_ref[...] = jnp.zeros_like(o_ref)

    def hidden(xb):
        hs = []
        for cs in _col_chunks(wg_ref.shape[1]):
            gate = jnp.dot(xb, wg_ref[:, cs], preferred_element_type=F32)
            up = jnp.dot(xb, wu_ref[:, cs], preferred_element_type=F32)
            hs.append((jax.nn.silu(gate) * up).astype(BF16))
        return jnp.concatenate(hs, axis=1)

    last = pl.num_programs(1) - 1

    @pl.when(k < last)
    def _():
        h = hidden(xb_ref[...])
        for cs in _col_chunks(o_ref.shape[1]):
            o_ref[:, cs] += jnp.dot(h, wd_ref[:, cs], preferred_element_type=F32)

    @pl.when(k == last)
    def _():
        for rs in _row_chunks(o_ref.shape[0]):
            y = o_ref[rs] + jnp.dot(hidden(xb_ref[rs]), wd_ref[...], preferred_element_type=F32)
            o_ref[rs] = _layer_norm(alpha * x_ref[rs] + y, g_ref[...], b_ref[...])


def _ffn_call(x, layer, w_gate, w_up, w_down, ln_g, ln_b, alpha, tm=1024, tf=512):
    n, d = x.shape
    dff = w_gate.shape[2]
    assert dff % tf == 0
    return pl.pallas_call(
        functools.partial(_ffn_kernel, alpha=alpha),
        grid=(n // tm, dff // tf),
        in_specs=[pl.BlockSpec((tm, d), lambda i, k: (i, 0)),
                  pl.BlockSpec((None, d, tf), lambda i, k: (layer, 0, k)),
                  pl.BlockSpec((None, d, tf), lambda i, k: (layer, 0, k)),
                  pl.BlockSpec((None, tf, d), lambda i, k: (layer, k, 0)),
                  pl.BlockSpec((1, d), lambda i, k: (0, 0)),
                  pl.BlockSpec((1, d), lambda i, k: (0, 0))],
        out_specs=pl.BlockSpec((tm, d), lambda i, k: (i, 0)),
        out_shape=jax.ShapeDtypeStruct((n, d), F32),
        scratch_shapes=[pltpu.VMEM((tm, d), BF16)],
        compiler_params=_compiler_params(("parallel", "arbitrary"), 60),
        name="ffn",
    )(x, w_gate, w_up, w_down, ln_g, ln_b)


def _trunk(x, w, rows, ln_in, alpha):
    batch, seq, d = x.shape
    h = x.reshape(batch * seq, d)
    for l, p in enumerate(rows):
        norm = ln_in if l == 0 else None
        qkv, u, gates = _proj_call(h, l, w["in"], w["gate"], p["b_gate"], ln_in=norm)
        a = _attn_call(qkv, p["bias"], batch, seq)
        f = _fourier_call(u, batch, seq)
        h = _mix_call(a, f, gates, h, l, w["att"], w["four"], w["out"],
                      p["ln1_g"], p["ln1_b"], alpha, ln_in=norm)
        h = _ffn_call(h, l, w["ffn_gate"], w["ffn_up"], w["ffn_down"], p["ln2_g"], p["ln2_b"], alpha)
    return h.reshape(batch, seq, d)


def kernel(x_prompt, x_sample, ln_in_g, ln_in_b, w_in, rpb, w_att, w_four, w_gate, b_gate, w_out,
           ln1_g, ln1_b, w_ffn_gate, w_ffn_up, w_ffn_down, ln2_g, ln2_b):
    depth = w_in.shape[0]
    alpha = (2.0 * depth) ** 0.25
    row = lambda v: v.reshape(1, -1)
    in_col_scale = np.ones((1, w_in.shape[2]), np.float32)
    in_col_scale[:, :ATT_WIDTH] = Q_SCALE
    w = {"in": _to_bf16(w_in, jnp.asarray(in_col_scale)), "gate": _to_bf16(w_gate),
         "att": _to_bf16(w_att), "four": _to_bf16(w_four), "out": _to_bf16(w_out),
         "ffn_gate": _to_bf16(w_ffn_gate), "ffn_up": _to_bf16(w_ffn_up),
         "ffn_down": _to_bf16(w_ffn_down)}
    rows = [dict(b_gate=row(b_gate[l]), bias=_bias_table(rpb[l]),
                 ln1_g=row(ln1_g[l]), ln1_b=row(ln1_b[l]), ln2_g=row(ln2_g[l]), ln2_b=row(ln2_b[l]))
            for l in range(depth)]
    ln_in = (row(ln_in_g), row(ln_in_b))
    return (_trunk(x_prompt, w, rows, ln_in, alpha), _trunk(x_sample, w, rows, ln_in, alpha))
```

```python
import functools

import numpy as np
import jax
import jax.numpy as jnp
from jax import lax
from jax.experimental import pallas as pl
from jax.experimental.pallas import tpu as pltpu

F32 = jnp.float32
BF16 = jnp.bfloat16

GRID_W = 64
N_HEADS = 8
HEAD_DIM = 128
ATT_WIDTH = N_HEADS * HEAD_DIM
KH = 8
KW = 16
F_GROUPS = 4
F_GROUP_DIM = 256
F_WIDTH = F_GROUPS * F_GROUP_DIM
LN_EPS = 1e-5
MASK_VALUE = -1e30
LOG2_E = 1.4426950408889634
Q_SCALE = HEAD_DIM ** -0.5 * LOG2_E

V7X_VMEM_BYTES = 64 * 1024 * 1024
MIB = 1024 * 1024

FAST = 128
ROW_BLOCK = 8
HALO_ROWS = KH // 2


def _compiler_params(semantics, vmem_mib):
    assert vmem_mib * MIB <= V7X_VMEM_BYTES
    return pltpu.CompilerParams(dimension_semantics=semantics, vmem_limit_bytes=vmem_mib * MIB)


def _resident(block_shape, index_map):
    return pl.BlockSpec(block_shape, index_map, pipeline_mode=pl.Buffered(1))


def _layer_norm(x, g, b):
    mu = jnp.mean(x, axis=-1, keepdims=True)
    xc = x - mu
    var = jnp.mean(xc * xc, axis=-1, keepdims=True)
    return xc * lax.rsqrt(var + LN_EPS) * g + b


CAST_ROWS = 256


def _cast_kernel(*refs, scaled):
    if scaled:
        w_ref, s_ref, o_ref = refs
        o_ref[...] = (w_ref[...] * s_ref[...]).astype(BF16)
    else:
        w_ref, o_ref = refs
        o_ref[...] = w_ref[...].astype(BF16)


def _to_bf16(w, col_scale=None):
    layers, r, c = w.shape
    assert r % CAST_ROWS == 0
    tile = pl.BlockSpec((None, CAST_ROWS, c), lambda l, i: (l, i, 0))
    scale_specs, scale_args = (([pl.BlockSpec((1, c), lambda l, i: (0, 0))], [col_scale])
                               if col_scale is not None else ([], []))
    return pl.pallas_call(
        functools.partial(_cast_kernel, scaled=col_scale is not None),
        grid=(layers, r // CAST_ROWS),
        in_specs=[tile] + scale_specs,
        out_specs=tile,
        out_shape=jax.ShapeDtypeStruct(w.shape, BF16),
        compiler_params=_compiler_params(("parallel", "parallel"), 32),
        name="cast",
    )(w, *scale_args)


MXU_COLS = 256
LN_ROWS = 256


def _col_chunks(width):
    return [slice(c, c + MXU_COLS) for c in range(0, width, MXU_COLS)]


def _row_chunks(nrows):
    return [slice(r, r + LN_ROWS) for r in range(0, nrows, LN_ROWS)]


def _proj_kernel(*refs, norm_input):
    if norm_input:
        x_ref, g_ref, b_ref, win_ref, wg_ref, bg_ref, qkv_ref, u_ref, gate_ref, xb_ref = refs
    else:
        x_ref, win_ref, wg_ref, bg_ref, qkv_ref, u_ref, gate_ref, xb_ref = refs
    first = pl.program_id(1) == 0

    @pl.when(first)
    def _():
        for rs in _row_chunks(x_ref.shape[0]):
            x = x_ref[rs]
            if norm_input:
                x = _layer_norm(x, g_ref[...], b_ref[...])
            xb_ref[rs] = x.astype(BF16)

    xb = xb_ref[...]
    for cs in _col_chunks(wg_ref.shape[1]):
        g = jnp.dot(xb, wg_ref[:, cs], preferred_element_type=F32) + bg_ref[:, cs]
        gate_ref[:, cs] = jax.nn.sigmoid(g).astype(BF16)
    for cs in _col_chunks(win_ref.shape[1]):
        qkv_ref[:, cs] = jnp.dot(xb, win_ref[:, cs], preferred_element_type=F32).astype(BF16)

    @pl.when(first)
    def _():
        u_ref[...] = qkv_ref[...]


def _proj_call(x, layer, w_in, w_gate, b_gate, ln_in=None, tm=1024, tn=1024):
    n, d = x.shape
    width = w_in.shape[2]
    steps = width // tn
    assert w_in.shape[1:] == (d, 3 * ATT_WIDTH + F_WIDTH) and w_gate.shape[1:] == (d, 2 * d)
    assert width == w_gate.shape[2] and width % tn == 0 and n % tm == 0 and tn == F_WIDTH
    const = pl.BlockSpec((1, d), lambda i, j: (0, 0))
    tile = pl.BlockSpec((tm, tn), lambda i, j: (i, j))
    ln_specs, ln_args = ([const, const], list(ln_in)) if ln_in is not None else ([], [])
    return pl.pallas_call(
        functools.partial(_proj_kernel, norm_input=ln_in is not None),
        grid=(n // tm, steps),
        in_specs=[pl.BlockSpec((tm, d), lambda i, j: (i, 0))] + ln_specs
                 + [pl.BlockSpec((None, d, tn), lambda i, j: (layer, 0, (j + steps - 1) % steps)),
                    pl.BlockSpec((None, d, tn), lambda i, j: (layer, 0, j)),
                    pl.BlockSpec((1, tn), lambda i, j: (0, j))],
        out_specs=[pl.BlockSpec((tm, tn), lambda i, j: (i, jnp.maximum(j - 1, 0))),
                   pl.BlockSpec((tm, tn), lambda i, j: (i, 0)),
                   tile],
        out_shape=[jax.ShapeDtypeStruct((n, width - tn), BF16),
                   jax.ShapeDtypeStruct((n, tn), BF16),
                   jax.ShapeDtypeStruct((n, 2 * d), BF16)],
        scratch_shapes=[pltpu.VMEM((tm, d), BF16)],
        compiler_params=_compiler_params(("parallel", "arbitrary"), 60),
        name="proj",
    )(x, *ln_args, w_in, w_gate, b_gate)


def _window_start_row(jb, rows):
    return jnp.clip(jb * ROW_BLOCK - HALO_ROWS, 0, rows - (ROW_BLOCK + 2 * HALO_ROWS))


def _attn_kernel(q_ref, k_ref, v_ref, bias_ref, o_ref, s_even, s_odd, p_scr, biasm, *, rows):
    jb = pl.program_id(1)
    win_tok = KH * GRID_W

    @pl.when((pl.program_id(0) == 0) & (jb == 0))
    def _():
        c = lax.broadcasted_iota(jnp.int32, (GRID_W, win_tok), 0)
        kc = lax.broadcasted_iota(jnp.int32, (GRID_W, win_tok), 1) & (GRID_W - 1)
        rel = kc - jnp.clip(c - KW // 2, 0, GRID_W - KW)
        valid = (rel >= 0) & (rel < KW)

        def mask_body(s, carry):
            for h in range(N_HEADS):
                biasm[s, h] = jnp.where(valid, bias_ref[s, h] * LOG2_E, MASK_VALUE)
            return carry

        lax.fori_loop(0, KH, mask_body, 0)

    heads = [slice(h * HEAD_DIM, (h + 1) * HEAD_DIM) for h in range(N_HEADS)]
    start_row = _window_start_row(jb, rows)

    def window(rr):
        r = jb * ROW_BLOCK + rr
        r_start = jnp.clip(r - KH // 2, 0, rows - KH)
        return (r_start - r + (KH - 1), pl.multiple_of((r_start - start_row) * GRID_W, GRID_W),
                pl.multiple_of(rr * GRID_W, GRID_W))

    def scores(rr, s_ref):
        _, kstart, qstart = window(rr)
        for h, hs in enumerate(heads):
            s_ref[h] = lax.dot_general(q_ref[pl.ds(qstart, GRID_W), hs],
                                       k_ref[pl.ds(kstart, win_tok), hs],
                                       (((1,), (1,)), ((), ())), preferred_element_type=F32)

    def attend(rr, s_ref):
        sidx, kstart, qstart = window(rr)
        ones = jnp.ones((win_tok, HEAD_DIM), BF16)
        for h, hs in enumerate(heads):
            s = s_ref[h] + biasm[sidx, h]
            p_scr[h] = jnp.exp2(s - jnp.max(s, axis=-1, keepdims=True)).astype(BF16)
            v1 = jnp.concatenate([v_ref[pl.ds(kstart, win_tok), hs], ones], axis=1)
            o = jnp.dot(p_scr[h], v1, preferred_element_type=F32)
            o_ref[pl.ds(qstart, GRID_W), hs] = (o[:, :HEAD_DIM] / o[:, HEAD_DIM:]).astype(BF16)

    scores(0, s_even)

    def pair_body(i, carry):
        rr = 2 * i
        scores(rr + 1, s_odd)
        attend(rr, s_even)
        scores(jnp.minimum(rr + 2, ROW_BLOCK - 1), s_even)
        attend(rr + 1, s_odd)
        return carry

    lax.fori_loop(0, ROW_BLOCK // 2, pair_body, 0, unroll=2)


def _bias_table(rpb):
    c = np.arange(GRID_W)[None, :, None]
    kc = np.arange(GRID_W)[None, None, :]
    col_sel = (np.arange(2 * KW - 1)[:, None, None] == kc - c + (KW - 1)).astype(np.float32)
    s = np.arange(KH)[:, None, None]
    i = np.arange(KH)[None, :, None]
    row_sel = (np.arange(2 * KH - 1)[None, None, :] == s + i).astype(np.float32)
    tabs = jnp.einsum("hdm,sid,mck->shcik", rpb, row_sel, col_sel, precision=lax.Precision.HIGHEST)
    return tabs.reshape(KH, N_HEADS, GRID_W, KH * GRID_W)


def _attn_call(qkv, bias, batch, seq):
    rows = seq // GRID_W
    fetch_rows = ROW_BLOCK + 2 * HALO_ROWS
    assert rows % ROW_BLOCK == 0 and rows >= fetch_rows
    body_tok = ROW_BLOCK * GRID_W
    score_shape = (N_HEADS, GRID_W, KH * GRID_W)

    def window(col_block):
        return pl.BlockSpec(
            (pl.Element(fetch_rows * GRID_W), pl.Element(ATT_WIDTH)),
            lambda b, j: ((b * rows + _window_start_row(j, rows)) * GRID_W, col_block * ATT_WIDTH))

    def body(b, j):
        return (b * (rows // ROW_BLOCK) + j, 0)

    return pl.pallas_call(
        functools.partial(_attn_kernel, rows=rows),
        grid=(batch, rows // ROW_BLOCK),
        in_specs=[pl.BlockSpec((body_tok, ATT_WIDTH), body), window(1), window(2),
                  _resident(bias.shape, lambda b, j: (0, 0, 0, 0))],
        out_specs=pl.BlockSpec((body_tok, ATT_WIDTH), body),
        out_shape=jax.ShapeDtypeStruct((batch * seq, ATT_WIDTH), BF16),
        scratch_shapes=[pltpu.VMEM(score_shape, F32), pltpu.VMEM(score_shape, F32),
                        pltpu.VMEM(score_shape, BF16), pltpu.VMEM(bias.shape, F32)],
        compiler_params=_compiler_params(("arbitrary", "arbitrary"), 48),
        name="attn",
    )(qkv, qkv, qkv, bias)


FOUR1_ROWS = 1024


def _four1_kernel(u_ref, cs_ref, g_ref, y_ref, *, nj):
    tfb = g_ref.shape[0]
    zz = []
    for g in range(F_GROUPS):
        ug = jnp.concatenate(
            [u_ref[0, :, t * F_WIDTH + g * F_GROUP_DIM:t * F_WIDTH + (g + 1) * F_GROUP_DIM]
             for t in range(tfb)], axis=0)
        zz.append(jnp.dot(ug, cs_ref[...], preferred_element_type=F32).astype(BF16))
    for t in range(tfb):
        rows = slice(t * nj, (t + 1) * nj)
        z = jnp.concatenate([jnp.concatenate([zg[rows, :F_GROUP_DIM] for zg in zz], axis=1),
                             jnp.concatenate([zg[rows, F_GROUP_DIM:] for zg in zz], axis=1)], axis=0)
        y = jnp.dot(g_ref[t], z, preferred_element_type=F32)
        y_ref[0, 0, :, t * F_WIDTH:(t + 1) * F_WIDTH] = y[:nj].astype(BF16)
        y_ref[0, 1, :, t * F_WIDTH:(t + 1) * F_WIDTH] = y[nj:].astype(BF16)


def _four2_kernel(y_ref, f_ref, p_ref, o_ref, *, scale):
    rs = []
    for l in range(ROW_BLOCK):
        yl = jnp.concatenate([y_ref[0, 0, l], y_ref[0, 1, l]], axis=0)
        rs.append((jnp.dot(f_ref[...], yl, preferred_element_type=F32) * scale).astype(BF16))
    r = jnp.dot(p_ref[...], jnp.concatenate(rs, axis=0), preferred_element_type=F32)
    o_ref[0] = r.reshape(FAST, ROW_BLOCK, F_WIDTH)


@functools.lru_cache(maxsize=None)
def _fourier_tables(seq):
    nj = seq // FAST
    cm = np.arange(F_GROUP_DIM, dtype=np.float64)
    ang = 2.0 * np.pi * np.outer(cm, cm) / F_GROUP_DIM
    chan = np.concatenate([np.cos(ang), np.sin(ang)], axis=1)

    tf = np.arange(FAST, dtype=np.float64)[:, None, None]
    klo = np.arange(nj, dtype=np.float64)[None, :, None]
    j = np.arange(nj, dtype=np.float64)[None, None, :]
    th = 2.0 * np.pi * ((klo * (FAST * j + tf)) % seq) / seq
    gc, gs = np.cos(th), np.sin(th)
    g = np.concatenate([np.concatenate([gc, -gs], axis=2),
                        np.concatenate([gs, gc], axis=2)], axis=1)

    khi = np.arange(FAST, dtype=np.float64)[:, None]
    tf2 = np.arange(FAST, dtype=np.float64)[None, :]
    ph = 2.0 * np.pi * ((khi * tf2) % FAST) / FAST
    f = np.concatenate([np.cos(ph), -np.sin(ph)], axis=1)
    place = np.einsum("hg,lm->hlmg", np.eye(FAST), np.eye(ROW_BLOCK)).reshape(
        FAST * ROW_BLOCK, ROW_BLOCK * FAST)
    return tuple(t.astype(np.float32) for t in (chan, g, f, place))


def _fourier_call(u, batch, seq):
    nj = seq // FAST
    tfb = FOUR1_ROWS // nj
    assert seq % FAST == 0 and nj % 16 == 0 and FOUR1_ROWS % nj == 0 and FAST % tfb == 0
    chan, g, f, place = (jnp.asarray(t).astype(BF16) for t in _fourier_tables(seq))
    lanes = tfb * F_WIDTH
    y = pl.pallas_call(
        functools.partial(_four1_kernel, nj=nj),
        grid=(batch, FAST // tfb),
        in_specs=[pl.BlockSpec((1, nj, lanes), lambda b, t: (b, 0, t)),
                  _resident(chan.shape, lambda b, t: (0, 0)),
                  pl.BlockSpec((tfb, 2 * nj, 2 * nj), lambda b, t: (t, 0, 0))],
        out_specs=pl.BlockSpec((1, 2, nj, lanes), lambda b, t: (b, 0, 0, t)),
        out_shape=jax.ShapeDtypeStruct((batch, 2, nj, FAST * F_WIDTH), BF16),
        compiler_params=_compiler_params(("parallel", "parallel"), 32),
        name="four1",
    )(u.reshape(batch, nj, FAST * F_WIDTH), chan, g)

    scale = float((seq * F_GROUP_DIM) ** -0.5)
    out = pl.pallas_call(
        functools.partial(_four2_kernel, scale=scale),
        grid=(batch, nj // ROW_BLOCK),
        in_specs=[pl.BlockSpec((1, 2, ROW_BLOCK, FAST, F_WIDTH), lambda b, k: (b, 0, k, 0, 0)),
                  _resident(f.shape, lambda b, k: (0, 0)),
                  _resident(place.shape, lambda b, k: (0, 0))],
        out_specs=pl.BlockSpec((1, FAST, ROW_BLOCK, F_WIDTH), lambda b, k: (b, 0, k, 0)),
        out_shape=jax.ShapeDtypeStruct((batch, FAST, nj, F_WIDTH), F32),
        compiler_params=_compiler_params(("parallel", "parallel"), 40),
        name="four2",
    )(y.reshape(batch, 2, nj, FAST, F_WIDTH), f, place)
    return out.reshape(batch * seq, F_WIDTH)


def _mix_kernel(*refs, alpha, norm_input):
    if norm_input:
        gi_ref, bi_ref, *refs = refs
    a_ref, f_ref, gate_ref, x_ref, watt_ref, wfour_ref, wout_ref, g_ref, b_ref, o_ref = refs
    d = x_ref.shape[1]
    for rs in _row_chunks(x_ref.shape[0]):
        a_in = a_ref[rs]
        f_in = f_ref[rs].astype(BF16)
        ms = []
        for cs in _col_chunks(d):
            a = jnp.dot(a_in, watt_ref[:, cs], preferred_element_type=F32)
            f = jnp.dot(f_in, wfour_ref[:, cs], preferred_element_type=F32)
            fs = slice(d + cs.start, d + cs.stop)
            ms.append((gate_ref[rs, cs].astype(F32) * a + gate_ref[rs, fs].astype(F32) * f).astype(BF16))
        y = jnp.dot(jnp.concatenate(ms, axis=1), wout_ref[...], preferred_element_type=F32)
        x = x_ref[rs]
        if norm_input:
            x = _layer_norm(x, gi_ref[...], bi_ref[...])
        o_ref[rs] = _layer_norm(alpha * x + y, g_ref[...], b_ref[...])


def _mix_call(a, f, gates, x, layer, w_att, w_four, w_out, ln_g, ln_b, alpha, ln_in=None, tm=512):
    n, d = x.shape
    row = lambda width: pl.BlockSpec((tm, width), lambda i: (i, 0))
    const = lambda shape: _resident(shape, lambda i: (0, 0))
    weight = lambda w: _resident((None,) + w.shape[1:], lambda i: (layer, 0, 0))
    ln_specs, ln_args = ([const((1, d))] * 2, list(ln_in)) if ln_in is not None else ([], [])
    return pl.pallas_call(
        functools.partial(_mix_kernel, alpha=alpha, norm_input=ln_in is not None),
        grid=(n // tm,),
        in_specs=ln_specs + [row(ATT_WIDTH), row(F_WIDTH), row(2 * d), row(d),
                             weight(w_att), weight(w_four), weight(w_out),
                             const((1, d)), const((1, d))],
        out_specs=row(d),
        out_shape=jax.ShapeDtypeStruct((n, d), F32),
        compiler_params=_compiler_params(("parallel",), 56),
        name="mix",
    )(*ln_args, a, f, gates, x, w_att, w_four, w_out, ln_g, ln_b)


def _ffn_kernel(x_ref, wg_ref, wu_ref, wd_ref, g_ref, b_ref, o_ref, xb_ref, *, alpha):
    k = pl.program_id(1)

    @pl.when(k == 0)
    def _():
        xb_ref[...] = x_ref[...].astype(BF16)
        o_ref[...] = jnp.zeros_like(o_ref)

    def hidden(xb):
        hs = []
        for cs in _col_chunks(wg_ref.shape[1]):
            gate = jnp.dot(xb, wg_ref[:, cs], preferred_element_type=F32)
            up = jnp.dot(xb, wu_ref[:, cs], preferred_element_type=F32)
            hs.append((jax.nn.silu(gate) * up).astype(BF16))
        return jnp.concatenate(hs, axis=1)

    last = pl.num_programs(1) - 1

    @pl.when(k < last)
    def _():
        h = hidden(xb_ref[...])
        for cs in _col_chunks(o_ref.shape[1]):
            o_ref[:, cs] += jnp.dot(h, wd_ref[:, cs], preferred_element_type=F32)

    @pl.when(k == last)
    def _():
        for rs in _row_chunks(o_ref.shape[0]):
            y = o_ref[rs] + jnp.dot(hidden(xb_ref[rs]), wd_ref[...], preferred_element_type=F32)
            o_ref[rs] = _layer_norm(alpha * x_ref[rs] + y, g_ref[...], b_ref[...])


def _ffn_call(x, layer, w_gate, w_up, w_down, ln_g, ln_b, alpha, tm=1024, tf=512):
    n, d = x.shape
    dff = w_gate.shape[2]
    assert dff % tf == 0
    return pl.pallas_call(
        functools.partial(_ffn_kernel, alpha=alpha),
        grid=(n // tm, dff // tf),
        in_specs=[pl.BlockSpec((tm, d), lambda i, k: (i, 0)),
                  pl.BlockSpec((None, d, tf), lambda i, k: (layer, 0, k)),
                  pl.BlockSpec((None, d, tf), lambda i, k: (layer, 0, k)),
                  pl.BlockSpec((None, tf, d), lambda i, k: (layer, k, 0)),
                  pl.BlockSpec((1, d), lambda i, k: (0, 0)),
                  pl.BlockSpec((1, d), lambda i, k: (0, 0))],
        out_specs=pl.BlockSpec((tm, d), lambda i, k: (i, 0)),
        out_shape=jax.ShapeDtypeStruct((n, d), F32),
        scratch_shapes=[pltpu.VMEM((tm, d), BF16)],
        compiler_params=_compiler_params(("parallel", "arbitrary"), 60),
        name="ffn",
    )(x, w_gate, w_up, w_down, ln_g, ln_b)


def _trunk(x, w, rows, ln_in, alpha):
    batch, seq, d = x.shape
    h = x.reshape(batch * seq, d)
    for l, p in enumerate(rows):
        norm = ln_in if l == 0 else None
        qkv, u, gates = _proj_call(h, l, w["in"], w["gate"], p["b_gate"], ln_in=norm)
        a = _attn_call(qkv, p["bias"], batch, seq)
        f = _fourier_call(u, batch, seq)
        h = _mix_call(a, f, gates, h, l, w["att"], w["four"], w["out"],
                      p["ln1_g"], p["ln1_b"], alpha, ln_in=norm)
        h = _ffn_call(h, l, w["ffn_gate"], w["ffn_up"], w["ffn_down"], p["ln2_g"], p["ln2_b"], alpha)
    return h.reshape(batch, seq, d)


def kernel(x_prompt, x_sample, ln_in_g, ln_in_b, w_in, rpb, w_att, w_four, w_gate, b_gate, w_out,
           ln1_g, ln1_b, w_ffn_gate, w_ffn_up, w_ffn_down, ln2_g, ln2_b):
    depth = w_in.shape[0]
    alpha = (2.0 * depth) ** 0.25
    row = lambda v: v.reshape(1, -1)
    in_col_scale = np.ones((1, w_in.shape[2]), np.float32)
    in_col_scale[:, :ATT_WIDTH] = Q_SCALE
    w = {"in": _to_bf16(w_in, jnp.asarray(in_col_scale)), "gate": _to_bf16(w_gate),
         "att": _to_bf16(w_att), "four": _to_bf16(w_four), "out": _to_bf16(w_out),
         "ffn_gate": _to_bf16(w_ffn_gate), "ffn_up": _to_bf16(w_ffn_up),
         "ffn_down": _to_bf16(w_ffn_down)}
    rows = [dict(b_gate=row(b_gate[l]), bias=_bias_table(rpb[l]),
                 ln1_g=row(ln1_g[l]), ln1_b=row(ln1_b[l]), ln2_g=row(ln2_g[l]), ln2_b=row(ln2_b[l]))
            for l in range(depth)]
    ln_in = (row(ln_in_g), row(ln_in_b))
    return (_trunk(x_prompt, w, rows, ln_in, alpha), _trunk(x_sample, w, rows, ln_in, alpha))
```

```python
import functools

import numpy as np
import jax
import jax.numpy as jnp
from jax import lax
from jax.experimental import pallas as pl
from jax.experimental.pallas import tpu as pltpu

F32 = jnp.float32
BF16 = jnp.bfloat16

GRID_W = 64
N_HEADS = 8
HEAD_DIM = 128
ATT_WIDTH = N_HEADS * HEAD_DIM
KH = 8
KW = 16
F_GROUPS = 4
F_GROUP_DIM = 256
F_WIDTH = F_GROUPS * F_GROUP_DIM
LN_EPS = 1e-5
MASK_VALUE = -1e30
LOG2_E = 1.4426950408889634
Q_SCALE = HEAD_DIM ** -0.5 * LOG2_E

V7X_VMEM_BYTES = 64 * 1024 * 1024
MIB = 1024 * 1024
VMEM_SMALL_MIB, VMEM_MEDIUM_MIB, VMEM_LARGE_MIB, VMEM_HUGE_MIB = 32, 48, 56, 60

FAST = 128
KLO_BLOCK = 8
ROW_BLOCK = 16
HALO_ROWS = KH // 2


def _compiler_params(semantics, vmem_mib):
    assert vmem_mib * MIB <= V7X_VMEM_BYTES
    return pltpu.CompilerParams(dimension_semantics=semantics, vmem_limit_bytes=vmem_mib * MIB)


def _resident(block_shape, index_map):
    return pl.BlockSpec(block_shape, index_map, pipeline_mode=pl.Buffered(1))


def _layer_norm(x, g, b):
    mu = jnp.mean(x, axis=-1, keepdims=True)
    xc = x - mu
    var = jnp.mean(xc * xc, axis=-1, keepdims=True)
    return xc * lax.rsqrt(var + LN_EPS) * g + b


CAST_ROWS = 256


def _cast_kernel(*refs, scaled):
    if scaled:
        w_ref, s_ref, o_ref = refs
        o_ref[...] = (w_ref[...] * s_ref[...]).astype(BF16)
    else:
        w_ref, o_ref = refs
        o_ref[...] = w_ref[...].astype(BF16)


def _to_bf16(w, col_scale=None):
    layers, r, c = w.shape
    assert r % CAST_ROWS == 0
    tile = pl.BlockSpec((None, CAST_ROWS, c), lambda l, i: (l, i, 0))
    scale_specs, scale_args = (([pl.BlockSpec((1, c), lambda l, i: (0, 0))], [col_scale])
                               if col_scale is not None else ([], []))
    return pl.pallas_call(
        functools.partial(_cast_kernel, scaled=col_scale is not None),
        grid=(layers, r // CAST_ROWS),
        in_specs=[tile] + scale_specs,
        out_specs=tile,
        out_shape=jax.ShapeDtypeStruct(w.shape, BF16),
        compiler_params=_compiler_params(("parallel", "parallel"), VMEM_SMALL_MIB),
        name="cast",
    )(w, *scale_args)


MXU_COLS = 256
LN_ROWS = 256


def _col_chunks(width):
    return [slice(c, c + MXU_COLS) for c in range(0, width, MXU_COLS)]


def _row_chunks(nrows):
    return [slice(r, r + LN_ROWS) for r in range(0, nrows, LN_ROWS)]


def _proj_kernel(*refs, norm_input):
    if norm_input:
        x_ref, g_ref, b_ref, win_ref, wg_ref, bg_ref, qkv_ref, u_ref, gate_ref, xb_ref = refs
    else:
        x_ref, win_ref, wg_ref, bg_ref, qkv_ref, u_ref, gate_ref, xb_ref = refs
    first = pl.program_id(1) == 0

    @pl.when(first)
    def _():
        for rs in _row_chunks(x_ref.shape[0]):
            x = x_ref[rs]
            if norm_input:
                x = _layer_norm(x, g_ref[...], b_ref[...])
            xb_ref[rs] = x.astype(BF16)

    xb = xb_ref[...]
    for cs in _col_chunks(wg_ref.shape[1]):
        g = jnp.dot(xb, wg_ref[:, cs], preferred_element_type=F32) + bg_ref[:, cs]
        gate_ref[:, cs] = jax.nn.sigmoid(g).astype(BF16)
    for cs in _col_chunks(win_ref.shape[1]):
        qkv_ref[:, cs] = jnp.dot(xb, win_ref[:, cs], preferred_element_type=F32).astype(BF16)

    @pl.when(first)
    def _():
        u_ref[...] = qkv_ref[...]


def _proj_call(x, layer, w_in, w_gate, b_gate, ln_in=None, tm=1024, tn=1024):
    n, d = x.shape
    width = w_in.shape[2]
    steps = width // tn
    assert w_in.shape[1:] == (d, 3 * ATT_WIDTH + F_WIDTH) and w_gate.shape[1:] == (d, 2 * d)
    assert width == w_gate.shape[2] and width % tn == 0 and n % tm == 0 and tn == F_WIDTH
    const = pl.BlockSpec((1, d), lambda i, j: (0, 0))
    tile = pl.BlockSpec((tm, tn), lambda i, j: (i, j))
    ln_specs, ln_args = ([const, const], list(ln_in)) if ln_in is not None else ([], [])
    return pl.pallas_call(
        functools.partial(_proj_kernel, norm_input=ln_in is not None),
        grid=(n // tm, steps),
        in_specs=[pl.BlockSpec((tm, d), lambda i, j: (i, 0))] + ln_specs
                 + [pl.BlockSpec((None, d, tn), lambda i, j: (layer, 0, (j + steps - 1) % steps)),
                    pl.BlockSpec((None, d, tn), lambda i, j: (layer, 0, j)),
                    pl.BlockSpec((1, tn), lambda i, j: (0, j))],
        out_specs=[pl.BlockSpec((tm, tn), lambda i, j: (i, jnp.maximum(j - 1, 0))),
                   pl.BlockSpec((tm, tn), lambda i, j: (i, 0)),
                   tile],
        out_shape=[jax.ShapeDtypeStruct((n, width - tn), BF16),
                   jax.ShapeDtypeStruct((n, tn), BF16),
                   jax.ShapeDtypeStruct((n, 2 * d), BF16)],
        scratch_shapes=[pltpu.VMEM((tm, d), BF16)],
        compiler_params=_compiler_params(("parallel", "arbitrary"), VMEM_HUGE_MIB),
        name="proj",
    )(x, *ln_args, w_in, w_gate, b_gate)


def _window_start_row(jb, rows):
    return jnp.clip(jb * ROW_BLOCK - HALO_ROWS, 0, rows - (ROW_BLOCK + 2 * HALO_ROWS))


def _attn_kernel(q_ref, k_ref, v_ref, bias_ref, o_ref, s_even, s_odd, p_scr, biasm, *, rows):
    jb = pl.program_id(1)
    win_tok = KH * GRID_W

    @pl.when((pl.program_id(0) == 0) & (jb == 0))
    def _():
        c = lax.broadcasted_iota(jnp.int32, (GRID_W, win_tok), 0)
        kc = lax.broadcasted_iota(jnp.int32, (GRID_W, win_tok), 1) & (GRID_W - 1)
        rel = kc - jnp.clip(c - KW // 2, 0, GRID_W - KW)
        valid = (rel >= 0) & (rel < KW)

        def mask_body(s, carry):
            for h in range(N_HEADS):
                biasm[s, h] = jnp.where(valid, bias_ref[s, h] * LOG2_E, MASK_VALUE)
            return carry

        lax.fori_loop(0, KH, mask_body, 0)

    heads = [slice(h * HEAD_DIM, (h + 1) * HEAD_DIM) for h in range(N_HEADS)]
    start_row = _window_start_row(jb, rows)

    def window(rr):
        r = jb * ROW_BLOCK + rr
        r_start = jnp.clip(r - KH // 2, 0, rows - KH)
        return (r_start - r + (KH - 1), pl.multiple_of((r_start - start_row) * GRID_W, GRID_W),
                pl.multiple_of(rr * GRID_W, GRID_W))

    def scores(rr, s_ref):
        _, kstart, qstart = window(rr)
        for h, hs in enumerate(heads):
            s_ref[h] = lax.dot_general(q_ref[pl.ds(qstart, GRID_W), hs],
                                       k_ref[pl.ds(kstart, win_tok), hs],
                                       (((1,), (1,)), ((), ())), preferred_element_type=F32)

    def attend(rr, s_ref):
        sidx, kstart, qstart = window(rr)
        for h, hs in enumerate(heads):
            s = s_ref[h] + biasm[sidx, h]
            p = jnp.exp2(s - jnp.max(s, axis=-1, keepdims=True))
            inv = 1.0 / jnp.sum(p, axis=-1, keepdims=True)
            p_scr[h] = p.astype(BF16)
            o = jnp.dot(p_scr[h], v_ref[pl.ds(kstart, win_tok), hs], preferred_element_type=F32)
            o_ref[pl.ds(qstart, GRID_W), hs] = (o * inv).astype(BF16)

    scores(0, s_even)

    def pair_body(i, carry):
        rr = 2 * i
        scores(rr + 1, s_odd)
        attend(rr, s_even)
        scores(jnp.minimum(rr + 2, ROW_BLOCK - 1), s_even)
        attend(rr + 1, s_odd)
        return carry

    lax.fori_loop(0, ROW_BLOCK // 2, pair_body, 0, unroll=2)


def _bias_table(rpb):
    c = np.arange(GRID_W)[None, :, None]
    kc = np.arange(GRID_W)[None, None, :]
    col_sel = (np.arange(2 * KW - 1)[:, None, None] == kc - c + (KW - 1)).astype(np.float32)
    s = np.arange(KH)[:, None, None]
    i = np.arange(KH)[None, :, None]
    row_sel = (np.arange(2 * KH - 1)[None, None, :] == s + i).astype(np.float32)
    tabs = jnp.einsum("hdm,sid,mck->shcik", rpb, row_sel, col_sel, precision=lax.Precision.HIGHEST)
    return tabs.reshape(KH, N_HEADS, GRID_W, KH * GRID_W)


def _attn_call(qkv, bias, batch, seq):
    rows = seq // GRID_W
    fetch_rows = ROW_BLOCK + 2 * HALO_ROWS
    assert rows % ROW_BLOCK == 0 and rows >= fetch_rows
    body_tok = ROW_BLOCK * GRID_W
    score_shape = (N_HEADS, GRID_W, KH * GRID_W)

    def window(col_block):
        return pl.BlockSpec(
            (pl.Element(fetch_rows * GRID_W), pl.Element(ATT_WIDTH)),
            lambda b, j: ((b * rows + _window_start_row(j, rows)) * GRID_W, col_block * ATT_WIDTH))

    def body(b, j):
        return (b * (rows // ROW_BLOCK) + j, 0)

    return pl.pallas_call(
        functools.partial(_attn_kernel, rows=rows),
        grid=(batch, rows // ROW_BLOCK),
        in_specs=[pl.BlockSpec((body_tok, ATT_WIDTH), body), window(1), window(2),
                  _resident(bias.shape, lambda b, j: (0, 0, 0, 0))],
        out_specs=pl.BlockSpec((body_tok, ATT_WIDTH), body),
        out_shape=jax.ShapeDtypeStruct((batch * seq, ATT_WIDTH), BF16),
        scratch_shapes=[pltpu.VMEM(score_shape, F32), pltpu.VMEM(score_shape, F32),
                        pltpu.VMEM(score_shape, BF16), pltpu.VMEM(bias.shape, F32)],
        compiler_params=_compiler_params(("arbitrary", "arbitrary"), VMEM_MEDIUM_MIB),
        name="attn",
    )(qkv, qkv, qkv, bias)


FOUR1_ROWS = 1024


def _four1_kernel(u_ref, cs_ref, g_ref, y_ref, *, nj):
    tfb = g_ref.shape[0]
    zz = []
    for g in range(F_GROUPS):
        ug = jnp.concatenate(
            [u_ref[0, :, t * F_WIDTH + g * F_GROUP_DIM:t * F_WIDTH + (g + 1) * F_GROUP_DIM]
             for t in range(tfb)], axis=0)
        zz.append(jnp.dot(ug, cs_ref[...], preferred_element_type=F32).astype(BF16))
    for t in range(tfb):
        rows = slice(t * nj, (t + 1) * nj)
        z = jnp.concatenate([jnp.concatenate([zg[rows, :F_GROUP_DIM] for zg in zz], axis=1),
                             jnp.concatenate([zg[rows, F_GROUP_DIM:] for zg in zz], axis=1)], axis=0)
        y = jnp.dot(g_ref[t], z, preferred_element_type=F32)
        y_ref[0, 0, :, t * F_WIDTH:(t + 1) * F_WIDTH] = y[:nj].astype(BF16)
        y_ref[0, 1, :, t * F_WIDTH:(t + 1) * F_WIDTH] = y[nj:].astype(BF16)


def _four2_kernel(y_ref, f_ref, p_ref, o_ref, *, scale):
    rs = []
    for l in range(KLO_BLOCK):
        yl = jnp.concatenate([y_ref[0, 0, l], y_ref[0, 1, l]], axis=0)
        rs.append((jnp.dot(f_ref[...], yl, preferred_element_type=F32) * scale).astype(BF16))
    r = jnp.dot(p_ref[...], jnp.concatenate(rs, axis=0), preferred_element_type=F32)
    o_ref[0] = r.reshape(FAST, KLO_BLOCK, F_WIDTH)


@functools.lru_cache(maxsize=None)
def _fourier_tables(seq):
    nj = seq // FAST
    cm = np.arange(F_GROUP_DIM, dtype=np.float64)
    ang = 2.0 * np.pi * np.outer(cm, cm) / F_GROUP_DIM
    chan = np.concatenate([np.cos(ang), np.sin(ang)], axis=1)

    tf = np.arange(FAST, dtype=np.float64)[:, None, None]
    klo = np.arange(nj, dtype=np.float64)[None, :, None]
    j = np.arange(nj, dtype=np.float64)[None, None, :]
    th = 2.0 * np.pi * ((klo * (FAST * j + tf)) % seq) / seq
    gc, gs = np.cos(th), np.sin(th)
    g = np.concatenate([np.concatenate([gc, -gs], axis=2),
                        np.concatenate([gs, gc], axis=2)], axis=1)

    khi = np.arange(FAST, dtype=np.float64)[:, None]
    tf2 = np.arange(FAST, dtype=np.float64)[None, :]
    ph = 2.0 * np.pi * ((khi * tf2) % FAST) / FAST
    f = np.concatenate([np.cos(ph), -np.sin(ph)], axis=1)
    place = np.einsum("hg,lm->hlmg", np.eye(FAST), np.eye(KLO_BLOCK)).reshape(
        FAST * KLO_BLOCK, KLO_BLOCK * FAST)
    return tuple(t.astype(np.float32) for t in (chan, g, f, place))


def _fourier_call(u, batch, seq):
    nj = seq // FAST
    tfb = FOUR1_ROWS // nj
    assert seq % FAST == 0 and nj % 16 == 0 and FOUR1_ROWS % nj == 0 and FAST % tfb == 0
    chan, g, f, place = (jnp.asarray(t).astype(BF16) for t in _fourier_tables(seq))
    lanes = tfb * F_WIDTH
    y = pl.pallas_call(
        functools.partial(_four1_kernel, nj=nj),
        grid=(batch, FAST // tfb),
        in_specs=[pl.BlockSpec((1, nj, lanes), lambda b, t: (b, 0, t)),
                  _resident(chan.shape, lambda b, t: (0, 0)),
                  pl.BlockSpec((tfb, 2 * nj, 2 * nj), lambda b, t: (t, 0, 0))],
        out_specs=pl.BlockSpec((1, 2, nj, lanes), lambda b, t: (b, 0, 0, t)),
        out_shape=jax.ShapeDtypeStruct((batch, 2, nj, FAST * F_WIDTH), BF16),
        compiler_params=_compiler_params(("parallel", "parallel"), VMEM_SMALL_MIB),
        name="four1",
    )(u.reshape(batch, nj, FAST * F_WIDTH), chan, g)

    scale = float((seq * F_GROUP_DIM) ** -0.5)
    out = pl.pallas_call(
        functools.partial(_four2_kernel, scale=scale),
        grid=(batch, nj // KLO_BLOCK),
        in_specs=[pl.BlockSpec((1, 2, KLO_BLOCK, FAST, F_WIDTH), lambda b, k: (b, 0, k, 0, 0)),
                  _resident(f.shape, lambda b, k: (0, 0)),
                  _resident(place.shape, lambda b, k: (0, 0))],
        out_specs=pl.BlockSpec((1, FAST, KLO_BLOCK, F_WIDTH), lambda b, k: (b, 0, k, 0)),
        out_shape=jax.ShapeDtypeStruct((batch, FAST, nj, F_WIDTH), F32),
        compiler_params=_compiler_params(("parallel", "parallel"), VMEM_MEDIUM_MIB),
        name="four2",
    )(y.reshape(batch, 2, nj, FAST, F_WIDTH), f, place)
    return out.reshape(batch * seq, F_WIDTH)


def _mix_kernel(*refs, alpha, norm_input):
    if norm_input:
        gi_ref, bi_ref, *refs = refs
    a_ref, f_ref, gate_ref, x_ref, watt_ref, wfour_ref, wout_ref, g_ref, b_ref, o_ref = refs
    d = x_ref.shape[1]
    for rs in _row_chunks(x_ref.shape[0]):
        a_in = a_ref[rs]
        f_in = f_ref[rs].astype(BF16)
        ms = []
        for cs in _col_chunks(d):
            a = jnp.dot(a_in, watt_ref[:, cs], preferred_element_type=F32)
            f = jnp.dot(f_in, wfour_ref[:, cs], preferred_element_type=F32)
            fs = slice(d + cs.start, d + cs.stop)
            ms.append((gate_ref[rs, cs].astype(F32) * a + gate_ref[rs, fs].astype(F32) * f).astype(BF16))
        y = jnp.dot(jnp.concatenate(ms, axis=1), wout_ref[...], preferred_element_type=F32)
        x = x_ref[rs]
        if norm_input:
            x = _layer_norm(x, gi_ref[...], bi_ref[...])
        o_ref[rs] = _layer_norm(alpha * x + y, g_ref[...], b_ref[...])


def _mix_call(a, f, gates, x, layer, w_att, w_four, w_out, ln_g, ln_b, alpha, ln_in=None, tm=512):
    n, d = x.shape
    row = lambda width: pl.BlockSpec((tm, width), lambda i: (i, 0))
    const = lambda shape: _resident(shape, lambda i: (0, 0))
    weight = lambda w: _resident((None,) + w.shape[1:], lambda i: (layer, 0, 0))
    ln_specs, ln_args = ([const((1, d))] * 2, list(ln_in)) if ln_in is not None else ([], [])
    return pl.pallas_call(
        functools.partial(_mix_kernel, alpha=alpha, norm_input=ln_in is not None),
        grid=(n // tm,),
        in_specs=ln_specs + [row(ATT_WIDTH), row(F_WIDTH), row(2 * d), row(d),
                             weight(w_att), weight(w_four), weight(w_out),
                             const((1, d)), const((1, d))],
        out_specs=row(d),
        out_shape=jax.ShapeDtypeStruct((n, d), F32),
        compiler_params=_compiler_params(("parallel",), VMEM_LARGE_MIB),
        name="mix",
    )(*ln_args, a, f, gates, x, w_att, w_four, w_out, ln_g, ln_b)


def _ffn_kernel(x_ref, wg_ref, wu_ref, wd_ref, g_ref, b_ref, o_ref, xb_ref, *, alpha):
    k = pl.program_id(1)

    @pl.when(k == 0)
    def _():
        xb_ref[...] = x_ref[...].astype(BF16)
        o_ref[...] = jnp.zeros_like(o_ref)

    def hidden(xb):
        hs = []
        for cs in _col_chunks(wg_ref.shape[1]):
            gate = jnp.dot(xb, wg_ref[:, cs], preferred_element_type=F32)
            up = jnp.dot(xb, wu_ref[:, cs], preferred_element_type=F32)
            hs.append((jax.nn.silu(gate) * up).astype(BF16))
        return jnp.concatenate(hs, axis=1)

    last = pl.num_programs(1) - 1

    @pl.when(k < last)
    def _():
        h = hidden(xb_ref[...])
        for cs in _col_chunks(o_ref.shape[1]):
            o_ref[:, cs] += jnp.dot(h, wd_ref[:, cs], preferred_element_type=F32)

    @pl.when(k == last)
    def _():
        for rs in _row_chunks(o_ref.shape[0]):
            y = o_ref[rs] + jnp.dot(hidden(xb_ref[rs]), wd_ref[...], preferred_element_type=F32)
            o_ref[rs] = _layer_norm(alpha * x_ref[rs] + y, g_ref[...], b_ref[...])


def _ffn_call(x, layer, w_gate, w_up, w_down, ln_g, ln_b, alpha, tm=1024, tf=512):
    n, d = x.shape
    dff = w_gate.shape[2]
    assert dff % tf == 0
    return pl.pallas_call(
        functools.partial(_ffn_kernel, alpha=alpha),
        grid=(n // tm, dff // tf),
        in_specs=[pl.BlockSpec((tm, d), lambda i, k: (i, 0)),
                  pl.BlockSpec((None, d, tf), lambda i, k: (layer, 0, k)),
                  pl.BlockSpec((None, d, tf), lambda i, k: (layer, 0, k)),
                  pl.BlockSpec((None, tf, d), lambda i, k: (layer, k, 0)),
                  pl.BlockSpec((1, d), lambda i, k: (0, 0)),
                  pl.BlockSpec((1, d), lambda i, k: (0, 0))],
        out_specs=pl.BlockSpec((tm, d), lambda i, k: (i, 0)),
        out_shape=jax.ShapeDtypeStruct((n, d), F32),
        scratch_shapes=[pltpu.VMEM((tm, d), BF16)],
        compiler_params=_compiler_params(("parallel", "arbitrary"), VMEM_HUGE_MIB),
        name="ffn",
    )(x, w_gate, w_up, w_down, ln_g, ln_b)


def _trunk(x, w, rows, ln_in, alpha):
    batch, seq, d = x.shape
    h = x.reshape(batch * seq, d)
    for l, p in enumerate(rows):
        norm = ln_in if l == 0 else None
        qkv, u, gates = _proj_call(h, l, w["in"], w["gate"], p["b_gate"], ln_in=norm)
        a = _attn_call(qkv, p["bias"], batch, seq)
        f = _fourier_call(u, batch, seq)
        h = _mix_call(a, f, gates, h, l, w["att"], w["four"], w["out"],
                      p["ln1_g"], p["ln1_b"], alpha, ln_in=norm)
        h = _ffn_call(h, l, w["ffn_gate"], w["ffn_up"], w["ffn_down"], p["ln2_g"], p["ln2_b"], alpha)
    return h.reshape(batch, seq, d)


def kernel(x_prompt, x_sample, ln_in_g, ln_in_b, w_in, rpb, w_att, w_four, w_gate, b_gate, w_out,
           ln1_g, ln1_b, w_ffn_gate, w_ffn_up, w_ffn_down, ln2_g, ln2_b):
    depth = w_in.shape[0]
    alpha = (2.0 * depth) ** 0.25
    row = lambda v: v.reshape(1, -1)
    in_col_scale = np.ones((1, w_in.shape[2]), np.float32)
    in_col_scale[:, :ATT_WIDTH] = Q_SCALE
    w = {"in": _to_bf16(w_in, jnp.asarray(in_col_scale)), "gate": _to_bf16(w_gate),
         "att": _to_bf16(w_att), "four": _to_bf16(w_four), "out": _to_bf16(w_out),
         "ffn_gate": _to_bf16(w_ffn_gate), "ffn_up": _to_bf16(w_ffn_up),
         "ffn_down": _to_bf16(w_ffn_down)}
    rows = [dict(b_gate=row(b_gate[l]), bias=_bias_table(rpb[l]),
                 ln1_g=row(ln1_g[l]), ln1_b=row(ln1_b[l]), ln2_g=row(ln2_g[l]), ln2_b=row(ln2_b[l]))
            for l in range(depth)]
    ln_in = (row(ln_in_g), row(ln_in_b))
    return (_trunk(x_prompt, w, rows, ln_in, alpha), _trunk(x_sample, w, rows, ln_in, alpha))
```
